```python
import jax
import jax.numpy as jnp
from jax import lax
import numpy as np

D_MODEL = 1024
BATCH = 8
SEQ = 2048
DEPTH = 4
DEC_BATCH = 32
DEC_SEQ = 8
PAST_LEN = 8192
PAGE_SIZE = 128

HEAD_DIM = 64
H_A = 8
W_A = H_A * HEAD_DIM
MOBA_BLOCK = 256
MOBA_TOPK = 3
H_C = 8
W_C = H_C * HEAD_DIM
POOL_WINDOWS = (2, 4, 8, 16)
N_POOL_GROUPS = len(POOL_WINDOWS)
POOL_GROUP_W = 128
W_B = N_POOL_GROUPS * POOL_GROUP_W
POOL_STATE = max(POOL_WINDOWS) - 1
N_BRANCH = 3
Q_BLOCK = 128
D_FF = -(-8 * D_MODEL // (3 * 256)) * 256
RMS_EPS = 1e-6
ATTN_SCALE = HEAD_DIM ** -0.5
FORGET_BIAS = 3.0
SPLIT_SIZES = (W_A, W_A, W_A, W_B, W_C, W_C, W_C, H_C, N_BRANCH * D_MODEL)
SPLIT_POINTS = tuple(int(v) for v in np.cumsum(SPLIT_SIZES)[:-1])
IN_W = int(sum(SPLIT_SIZES))

kernel_name = 'hybrid_moba_pool_fox_decode_step'


def rmsnorm(x, g):
    xf = x.astype(jnp.float32)
    xf = xf * lax.rsqrt(jnp.mean(xf * xf, axis=-1, keepdims=True) + RMS_EPS)
    return xf.astype(x.dtype) * g


def project(h, w_in, b_f):
    bsz, t, _ = h.shape
    z = h @ w_in
    qa, ka, va, u, qc, kc, vc, fr, gr = jnp.split(z, SPLIT_POINTS, axis=-1)
    heads = lambda a, nh: a.reshape(bsz, t, nh, HEAD_DIM)
    logf = jax.nn.log_sigmoid((fr + b_f).astype(jnp.float32))
    gates = jax.nn.sigmoid(gr).reshape(bsz, t, N_BRANCH, D_MODEL)
    return (heads(qa, H_A), heads(ka, H_A), heads(va, H_A), u,
            heads(qc, H_C), heads(kc, H_C), heads(vc, H_C), logf, gates)


def moba_blocks(k, v):
    bsz, length, nh, hd = k.shape
    nb = max(-(-length // MOBA_BLOCK), MOBA_TOPK + 1)
    pad = ((0, 0), (0, nb * MOBA_BLOCK - length), (0, 0), (0, 0))
    kb = jnp.pad(k, pad).reshape(bsz, nb, MOBA_BLOCK, nh, hd).transpose(0, 3, 1, 2, 4)
    vb = jnp.pad(v, pad).reshape(bsz, nb, MOBA_BLOCK, nh, hd).transpose(0, 3, 1, 2, 4)
    kbar = jnp.mean(kb.astype(jnp.float32), axis=3)
    return kb, vb, kbar


def moba_chunk(q, q_pos, kb, vb, kbar):
    tq, nh, _ = q.shape
    nb = kb.shape[1]
    own = q_pos // MOBA_BLOCK
    s = jnp.einsum('qhd,hnd->qhn', q.astype(jnp.float32), kbar)
    s = jnp.where(jnp.arange(nb)[None, None, :] < own[:, None, None], s, -jnp.inf)
    _, top = lax.top_k(s, MOBA_TOPK)
    own_b = jnp.broadcast_to(own[:, None, None], (tq, nh, 1))
    sel = jnp.concatenate([top, own_b], axis=-1)
    valid = jnp.concatenate([top < own[:, None, None], jnp.ones((tq, nh, 1), bool)], axis=-1)
    hidx = jnp.arange(nh)[None, :, None]
    gk = kb[hidx, sel]
    gv = vb[hidx, sel]
    kpos = sel[..., None] * MOBA_BLOCK + jnp.arange(MOBA_BLOCK)
    mask = valid[..., None] & (kpos <= q_pos[:, None, None, None])
    logits = jnp.einsum('qhd,qhnbd->qhnb', q, gk).astype(jnp.float32) * ATTN_SCALE
    logits = jnp.where(mask, logits, -jnp.inf)
    p = jax.nn.softmax(logits.reshape(tq, nh, -1), axis=-1).reshape(logits.shape)
    return jnp.einsum('qhnb,qhnbd->qhd', p.astype(gv.dtype), gv)


def moba_prompt(q, k, v):
    bsz, t, nh, hd = q.shape
    kb, vb, kbar = moba_blocks(k, v)
    nqb = t // Q_BLOCK
    qc = q.reshape(bsz, nqb, Q_BLOCK, nh, hd)
    pos = jnp.arange(t, dtype=jnp.int32).reshape(nqb, Q_BLOCK)

    def per_seq(args):
        q_s, kb_s, vb_s, kbar_s = args
        return lax.map(lambda a: moba_chunk(a[0], a[1], kb_s, vb_s, kbar_s), (q_s, pos))

    o = lax.map(per_seq, (qc, kb, vb, kbar))
    return o.reshape(bsz, t, nh * hd)


def moba_sample(q, k_all, v_all, q_pos):
    bsz, tq, nh, hd = q.shape
    kb, vb, kbar = moba_blocks(k_all, v_all)
    o = lax.map(lambda a: moba_chunk(a[0], q_pos, a[1], a[2], a[3]), (q, kb, vb, kbar))
    return o.reshape(bsz, tq, nh * hd)


def forget_bias(logf):
    logf = logf.astype(jnp.float32)
    return logf - lax.cumsum(logf, axis=1, reverse=True)


def fox_attend(q, k, v, gq, gk, q_pos, k_pos):
    s = jnp.einsum('bqhd,bkhd->bhqk', q, k).astype(jnp.float32) * ATTN_SCALE
    s = s + gq.transpose(0, 2, 1)[..., :, None] - gk.transpose(0, 2, 1)[..., None, :]
    s = jnp.where(k_pos[None, :] <= q_pos[:, None], s, -jnp.inf)
    p = jax.nn.softmax(s, axis=-1)
    return jnp.einsum('bhqk,bkhd->bqhd', p.astype(v.dtype), v)


def fox_prompt(q, k, v, logf):
    bsz, t, nh, hd = q.shape
    g = forget_bias(logf)
    nqb = t // Q_BLOCK
    pos = jnp.arange(t, dtype=jnp.int32)
    blk = lambda a: a.reshape((bsz, nqb, Q_BLOCK) + a.shape[2:]).swapaxes(0, 1)
    o = lax.map(lambda a: fox_attend(a[0], k, v, a[1], g, a[2], pos),
                (blk(q), blk(g), pos.reshape(nqb, Q_BLOCK)))
    return o.swapaxes(0, 1).reshape(bsz, t, nh * hd)


def fox_sample(q, k_all, v_all, logf_all, q_pos):
    bsz, tq, nh, hd = q.shape
    g = forget_bias(logf_all)
    k_pos = jnp.arange(k_all.shape[1], dtype=jnp.int32)
    o = fox_attend(q, k_all, v_all, g[:, -tq:], g, q_pos, k_pos)
    return o.reshape(bsz, tq, nh * hd)


def pool_mix(u_ext, pos_ext, n_out, w_pool, pool_scale):
    bsz, le, _ = u_ext.shape
    uf = u_ext.astype(jnp.float32)
    c = jnp.pad(jnp.cumsum(uf, axis=1), ((0, 0), (1, 0), (0, 0)))
    parts = []
    for gi, w in enumerate(POOL_WINDOWS):
        sl = slice(gi * POOL_GROUP_W, (gi + 1) * POOL_GROUP_W)
        cg = c[..., sl]
        lo = jnp.pad(cg, ((0, 0), (w, 0), (0, 0)))[:, 1:le + 1]
        cnt = jnp.minimum(w, pos_ext + 1).astype(jnp.float32)[None, :, None]
        parts.append((cg[:, 1:] - lo) / cnt - uf[..., sl])
    d = jnp.stack(parts, axis=2)[:, -n_out:]
    y = jnp.einsum('btgc,gcd->btgd', d.astype(w_pool.dtype), w_pool)
    return (y.reshape(bsz, n_out, W_B) * pool_scale).astype(u_ext.dtype)


def merge(gates, a, b, c, w_br_a, w_br_b, w_br_c, w_out):
    y = (gates[..., 0, :] * (a @ w_br_a) + gates[..., 1, :] * (b @ w_br_b)
         + gates[..., 2, :] * (c @ w_br_c))
    return y @ w_out


def swiglu(h, w_ffn_in, w_ffn_out):
    gt, up = jnp.split(h @ w_ffn_in, 2, axis=-1)
    return (jax.nn.silu(gt) * up) @ w_ffn_out


def setup_inputs(seed: int = 0) -> dict:
    key = jax.random.key(seed)
    ks = jax.random.split(key, 24)
    f32 = jnp.float32
    n_pages = PAST_LEN // PAGE_SIZE
    n_used = DEC_BATCH * n_pages
    n_phys = n_used + -(-n_used // 4)

    def nrm(k, shape, scale=1.0):
        return scale * jax.random.normal(k, shape, f32)

    x_prompt = nrm(ks[0], (BATCH, SEQ, D_MODEL))
    x_sample = nrm(ks[1], (DEC_BATCH, DEC_SEQ, D_MODEL))
    cache_a_k = nrm(ks[2], (DEPTH, n_phys, PAGE_SIZE, H_A, HEAD_DIM))
    cache_a_v = nrm(ks[3], (DEPTH, n_phys, PAGE_SIZE, H_A, HEAD_DIM))
    cache_c_k = nrm(ks[4], (DEPTH, n_phys, PAGE_SIZE, H_C, HEAD_DIM))
    cache_c_v = nrm(ks[5], (DEPTH, n_phys, PAGE_SIZE, H_C, HEAD_DIM))
    cache_c_logf = jax.nn.log_sigmoid(FORGET_BIAS + nrm(ks[6], (DEPTH, n_phys, PAGE_SIZE, H_C)))
    state_pool = nrm(ks[7], (DEPTH, DEC_BATCH, POOL_STATE, W_B))
    page_table = jax.random.permutation(ks[8], n_phys)[:n_used].reshape(DEC_BATCH, n_pages).astype(jnp.int32)
    norm1 = 1.0 + nrm(ks[9], (DEPTH, D_MODEL), 0.05)
    w_in = nrm(ks[10], (DEPTH, D_MODEL, IN_W), D_MODEL ** -0.5)
    b_f = FORGET_BIAS + nrm(ks[11], (DEPTH, H_C), 0.1)
    w_pool = nrm(ks[12], (DEPTH, N_POOL_GROUPS, POOL_GROUP_W, POOL_GROUP_W), POOL_GROUP_W ** -0.5)
    pool_scale = 1.0 + nrm(ks[13], (DEPTH, W_B), 0.1)
    w_br_a = nrm(ks[14], (DEPTH, W_A, D_MODEL), W_A ** -0.5)
    w_br_b = nrm(ks[15], (DEPTH, W_B, D_MODEL), W_B ** -0.5)
    w_br_c = nrm(ks[16], (DEPTH, W_C, D_MODEL), W_C ** -0.5)
    w_out = nrm(ks[17], (DEPTH, D_MODEL, D_MODEL), D_MODEL ** -0.5)
    norm2 = 1.0 + nrm(ks[18], (DEPTH, D_MODEL), 0.05)
    w_ffn_in = nrm(ks[19], (DEPTH, D_MODEL, 2 * D_FF), D_MODEL ** -0.5)
    w_ffn_out = nrm(ks[20], (DEPTH, D_FF, D_MODEL), D_FF ** -0.5)
    norm_f = 1.0 + nrm(ks[21], (D_MODEL,), 0.05)
    return {'x_prompt': x_prompt, 'x_sample': x_sample,
            'cache_a_k': cache_a_k, 'cache_a_v': cache_a_v,
            'cache_c_k': cache_c_k, 'cache_c_v': cache_c_v, 'cache_c_logf': cache_c_logf,
            'state_pool': state_pool, 'page_table': page_table,
            'norm1': norm1, 'w_in': w_in, 'b_f': b_f, 'w_pool': w_pool, 'pool_scale': pool_scale,
            'w_br_a': w_br_a, 'w_br_b': w_br_b, 'w_br_c': w_br_c, 'w_out': w_out,
            'norm2': norm2, 'w_ffn_in': w_ffn_in, 'w_ffn_out': w_ffn_out, 'norm_f': norm_f}


def reference(x_prompt, x_sample, cache_a_k, cache_a_v, cache_c_k, cache_c_v, cache_c_logf,
              state_pool, page_table, norm1, w_in, b_f, w_pool, pool_scale, w_br_a, w_br_b,
              w_br_c, w_out, norm2, w_ffn_in, w_ffn_out, norm_f):
    n_pages = PAST_LEN // PAGE_SIZE
    t_p = x_prompt.shape[1]
    t_s = x_sample.shape[1]
    pos_p = jnp.arange(t_p, dtype=jnp.int32)
    q_pos_s = PAST_LEN + jnp.arange(t_s, dtype=jnp.int32)
    pool_pos_s = PAST_LEN - POOL_STATE + jnp.arange(POOL_STATE + t_s, dtype=jnp.int32)

    def paged(cache_l):
        rows = cache_l[page_table]
        return rows.reshape((rows.shape[0], n_pages * PAGE_SIZE) + rows.shape[3:])

    yp, ys = x_prompt, x_sample
    pak, pav, pck, pcv, pcf, ppl = [], [], [], [], [], []
    sak, sav, sck, scv, scf, spl = [], [], [], [], [], []
    for l in range(DEPTH):
        h = rmsnorm(yp, norm1[l])
        qa, ka, va, u, qc, kc, vc, logf, gates = project(h, w_in[l], b_f[l])
        a = moba_prompt(qa, ka, va)
        b = pool_mix(u, pos_p, t_p, w_pool[l], pool_scale[l])
        c = fox_prompt(qc, kc, vc, logf)
        yp = yp + merge(gates, a, b, c, w_br_a[l], w_br_b[l], w_br_c[l], w_out[l])
        yp = yp + swiglu(rmsnorm(yp, norm2[l]), w_ffn_in[l], w_ffn_out[l])
        pak.append(ka); pav.append(va); pck.append(kc); pcv.append(vc); pcf.append(logf)
        ppl.append(u[:, -POOL_STATE:])

        h = rmsnorm(ys, norm1[l])
        qa, ka, va, u, qc, kc, vc, logf, gates = project(h, w_in[l], b_f[l])
        ka_all = jnp.concatenate([paged(cache_a_k[l]), ka], axis=1)
        va_all = jnp.concatenate([paged(cache_a_v[l]), va], axis=1)
        a = moba_sample(qa, ka_all, va_all, q_pos_s)
        u_ext = jnp.concatenate([state_pool[l].astype(u.dtype), u], axis=1)
        b = pool_mix(u_ext, pool_pos_s, t_s, w_pool[l], pool_scale[l])
        kc_all = jnp.concatenate([paged(cache_c_k[l]), kc], axis=1)
        vc_all = jnp.concatenate([paged(cache_c_v[l]), vc], axis=1)
        lf_all = jnp.concatenate([paged(cache_c_logf[l]).astype(jnp.float32), logf], axis=1)
        c = fox_sample(qc, kc_all, vc_all, lf_all, q_pos_s)
        ys = ys + merge(gates, a, b, c, w_br_a[l], w_br_b[l], w_br_c[l], w_out[l])
        ys = ys + swiglu(rmsnorm(ys, norm2[l]), w_ffn_in[l], w_ffn_out[l])
        sak.append(ka); sav.append(va); sck.append(kc); scv.append(vc); scf.append(logf)
        spl.append(u_ext[:, -POOL_STATE:])

    y_prompt = rmsnorm(yp, norm_f)
    y_sample = rmsnorm(ys, norm_f)
    new_a_k_p = jnp.stack(pak)
    new_a_v_p = jnp.stack(pav)
    new_c_k_p = jnp.stack(pck)
    new_c_v_p = jnp.stack(pcv)
    new_c_logf_p = jnp.stack(pcf)
    new_pool_p = jnp.stack(ppl)
    new_a_k_s = jnp.stack(sak)
    new_a_v_s = jnp.stack(sav)
    new_c_k_s = jnp.stack(sck)
    new_c_v_s = jnp.stack(scv)
    new_c_logf_s = jnp.stack(scf)
    new_pool_s = jnp.stack(spl)
    return (y_prompt, y_sample, new_a_k_p, new_a_v_p, new_c_k_p, new_c_v_p, new_c_logf_p, new_pool_p,
            new_a_k_s, new_a_v_s, new_c_k_s, new_c_v_s, new_c_logf_s, new_pool_s)
```

```python
import functools

import jax
import jax.numpy as jnp
from jax import lax
from jax.experimental import pallas as pl
from jax.experimental.pallas import tpu as pltpu

F32 = jnp.float32
BF16 = jnp.bfloat16

HEAD_DIM = 64
PAIR_W = 2 * HEAD_DIM
MOBA_BLOCK = 256
MOBA_TOPK = 3
POOL_WINDOWS = (2, 4, 8, 16)
POOL_GROUP_W = 128
POOL_HIST = max(POOL_WINDOWS)
PAGE_SIZE = 128
RMS_EPS = 1e-6
ATTN_SCALE = HEAD_DIM ** -0.5
NEG = -1e30
QTILE = 256
PAGES_PER_STEP = 16
V7X_VMEM_LIMIT = 56 * 1024 * 1024


def _params(sem):
    return pltpu.CompilerParams(dimension_semantics=sem, vmem_limit_bytes=V7X_VMEM_LIMIT)


def _rmsnorm(x, g):
    ms = jnp.mean(x * x, axis=-1, keepdims=True)
    return x * lax.rsqrt(ms + RMS_EPS) * g


def _dot(a, b):
    return jnp.dot(a, b, preferred_element_type=F32)


def _dot_nt(a, b, precision=None):
    return lax.dot_general(a, b, (((1,), (1,)), ((), ())), preferred_element_type=F32, precision=precision)


def _dot_tn(a, b):
    return lax.dot_general(a, b, (((0,), (0,)), ((), ())), preferred_element_type=F32)


def _split3(x):
    hi = x.astype(BF16)
    r = x - hi.astype(F32)
    mid = r.astype(BF16)
    lo = (r - mid.astype(F32)).astype(BF16)
    return hi, mid, lo


def _proj_kernel(*refs, n_alias, n_main):
    x_ref, g_ref, w_ref, wf_ref, bf_ref = refs[:5]
    outs = refs[5 + n_alias:]
    main_outs, lf_ref = outs[:n_main], outs[n_main]
    hb = _rmsnorm(x_ref[...], g_ref[...]).astype(BF16)
    width = main_outs[0].shape[-1]
    for j, o_ref in enumerate(main_outs):
        o_ref[...] = _dot(hb, w_ref[:, j * width:(j + 1) * width])
    fr = _dot(hb, wf_ref[...]) + bf_ref[...]
    lf_ref[...] = -(jnp.maximum(-fr, 0.0) + jnp.log1p(jnp.exp(-jnp.abs(fr))))


def _project(x, g, w_main, w_f, b_f, stacked, layer, depth, tm):
    m, d = x.shape
    width = w_main.shape[1] // 7
    nh = w_f.shape[1]
    row = lambda i: (i, 0)
    const = lambda i: (0, 0)
    lay = lambda i: (layer, i, 0)
    in_specs = [pl.BlockSpec((tm, d), row), pl.BlockSpec((1, d), const), pl.BlockSpec(w_main.shape, const),
                pl.BlockSpec(w_f.shape, const), pl.BlockSpec((1, nh), const)]
    args = [x, g, w_main, w_f, b_f]
    aliases = {}
    if stacked is not None:
        in_specs += [pl.BlockSpec(memory_space=pl.ANY)] * 5
        args += list(stacked)
        aliases = {5: 1, 6: 2, 7: 5, 8: 6, 9: 7}
    plain = pl.BlockSpec((tm, width), row)
    stack = pl.BlockSpec((None, tm, width), lay)
    out_specs = [plain, stack, stack, plain, plain, stack, stack, pl.BlockSpec((None, tm, nh), lay)]
    plain_s = jax.ShapeDtypeStruct((m, width), F32)
    stack_s = jax.ShapeDtypeStruct((depth, m, width), F32)
    out_shape = [plain_s, stack_s, stack_s, plain_s, plain_s, stack_s, stack_s,
                 jax.ShapeDtypeStruct((depth, m, nh), F32)]
    qa, ka, va, u, qc, kc, vc, lf = pl.pallas_call(
        functools.partial(_proj_kernel, n_alias=len(aliases), n_main=7),
        grid=(m // tm,), in_specs=in_specs, out_specs=out_specs, out_shape=out_shape,
        input_output_aliases=aliases, compiler_params=_params(("arbitrary",)), name="proj",
    )(*args)
    return qa, u, qc, (ka, va, kc, vc, lf)


def _merge_kernel(x_ref, g_ref, a_ref, b_ref, c_ref, wg_ref, wa_ref, wb_ref, wc_ref, wo_ref, o_ref):
    x = x_ref[...]
    d = x.shape[-1]
    hb = _rmsnorm(x, g_ref[...]).astype(BF16)
    y = None
    for i, (br_ref, w_ref) in enumerate(((a_ref, wa_ref), (b_ref, wb_ref), (c_ref, wc_ref))):
        gate = jax.nn.sigmoid(_dot(hb, wg_ref[:, i * d:(i + 1) * d]))
        t = gate * _dot(br_ref[...], w_ref[...])
        y = t if y is None else y + t
    o_ref[...] = x + _dot(y.astype(BF16), wo_ref[...])


def _merge(x, g, a, b, c, w_g, w_a, w_b, w_c, w_o, tm):
    m, d = x.shape
    row = lambda i: (i, 0)
    const = lambda i: (0, 0)
    full = lambda arr: pl.BlockSpec(arr.shape, const)
    br = lambda arr: pl.BlockSpec((tm, arr.shape[1]), row)
    return pl.pallas_call(
        _merge_kernel, grid=(m // tm,),
        in_specs=[pl.BlockSpec((tm, d), row), pl.BlockSpec((1, d), const), br(a), br(b), br(c),
                  full(w_g), full(w_a), full(w_b), full(w_c), full(w_o)],
        out_specs=pl.BlockSpec((tm, d), row), out_shape=jax.ShapeDtypeStruct((m, d), F32),
        compiler_params=_params(("arbitrary",)), name="merge",
    )(x, g, a, b, c, w_g, w_a, w_b, w_c, w_o)


def _ffn_kernel(x_ref, g_ref, wi_ref, wo_ref, gf_ref, o_ref, *, chunk, final_norm):
    x = x_ref[...]
    d_ff = wo_ref.shape[0]
    hb = _rmsnorm(x, g_ref[...]).astype(BF16)
    acc = x
    for c0 in range(0, d_ff, chunk):
        gt = _dot(hb, wi_ref[:, c0:c0 + chunk])
        up = _dot(hb, wi_ref[:, d_ff + c0:d_ff + c0 + chunk])
        act = (gt * jax.nn.sigmoid(gt) * up).astype(BF16)
        acc = acc + _dot(act, wo_ref[c0:c0 + chunk, :])
    o_ref[...] = _rmsnorm(acc, gf_ref[...]) if final_norm else acc


def _ffn(x, g, w_i, w_o, g_final, final_norm, tm, chunk=256):
    m, d = x.shape
    row = lambda i: (i, 0)
    const = lambda i: (0, 0)
    return pl.pallas_call(
        functools.partial(_ffn_kernel, chunk=chunk, final_norm=final_norm), grid=(m // tm,),
        in_specs=[pl.BlockSpec((tm, d), row), pl.BlockSpec((1, d), const), pl.BlockSpec(w_i.shape, const),
                  pl.BlockSpec(w_o.shape, const), pl.BlockSpec((1, d), const)],
        out_specs=pl.BlockSpec((tm, d), row), out_shape=jax.ShapeDtypeStruct((m, d), F32),
        compiler_params=_params(("arbitrary",)), name="ffn",
    )(x, g, w_i, w_o, g_final)


def _pool_kernel(u_ref, hist_ref, w_ref, scale_ref, o_ref, s_ref, *, pos0):
    nb, t, _ = u_ref.shape
    ext = t + POOL_HIST
    pos = pos0 + lax.broadcasted_iota(jnp.int32, (1, t, 1), 1)
    for gi, win in enumerate(POOL_WINDOWS):
        sl = slice(gi * POOL_GROUP_W, (gi + 1) * POOL_GROUP_W)
        u = u_ref[:, :, sl]
        s_ref[:, 0:POOL_HIST, :] = hist_ref[:, :, sl]
        s_ref[:, POOL_HIST:, :] = u
        shift = 1
        while shift < win:
            s_ref[:, shift:, :] = s_ref[:, shift:, :] + s_ref[:, 0:ext - shift, :]
            shift *= 2
        cnt = jnp.minimum(win, pos + 1).astype(F32)
        dlt = s_ref[:, POOL_HIST:, :] / cnt - u
        y = _dot(dlt.reshape(nb * t, POOL_GROUP_W).astype(BF16), w_ref[gi])
        o_ref[:, :, sl] = (y.reshape(nb, t, POOL_GROUP_W) * scale_ref[:, sl]).astype(o_ref.dtype)


def _pool(u, hist, w_pool, scale, pos0, nb):
    b, t, w = u.shape
    blk = lambda i: (i, 0, 0)
    return pl.pallas_call(
        functools.partial(_pool_kernel, pos0=pos0), grid=(b // nb,),
        in_specs=[pl.BlockSpec((nb, t, w), blk), pl.BlockSpec((nb, POOL_HIST, w), blk),
                  pl.BlockSpec(w_pool.shape, lambda i: (0, 0, 0)), pl.BlockSpec((1, w), lambda i: (0, 0))],
        out_specs=pl.BlockSpec((nb, t, w), blk), out_shape=jax.ShapeDtypeStruct((b, t, w), BF16),
        scratch_shapes=[pltpu.VMEM((nb, t + POOL_HIST, POOL_GROUP_W), F32)],
        compiler_params=_params(("arbitrary",)), name="pool",
    )(u, hist, w_pool, scale)


def _fox_prep_kernel(lf_ref, kx_ref):
    t, nh = lf_ref.shape
    r = lax.broadcasted_iota(jnp.int32, (QTILE, QTILE), 0)
    c = lax.broadcasted_iota(jnp.int32, (QTILE, QTILE), 1)
    tri = (r >= c).astype(BF16)
    carry = jnp.zeros((1, nh), F32)
    parts = []
    for blk in range(t // QTILE):
        hi, mid, lo = _split3(lf_ref[blk * QTILE:(blk + 1) * QTILE, :])
        cs = _dot(tri, hi) + _dot(tri, mid) + _dot(tri, lo) + carry
        carry = cs[QTILE - 1:QTILE, :]
        parts.append(cs)
    neg_c = -jnp.concatenate(parts, axis=0)
    n_pair = nh // 2
    h_idx = lax.broadcasted_iota(jnp.int32, (nh, n_pair * PAIR_W), 0)
    col = lax.broadcasted_iota(jnp.int32, (nh, n_pair * PAIR_W), 1)
    pair, lane = col // PAIR_W, col % PAIR_W
    kx = None
    for part, term in enumerate(_split3(neg_c)):
        sel = ((h_idx == 2 * pair) & (lane == part)) | ((h_idx == 2 * pair + 1) & (lane == 3 + part))
        placed = _dot(term, sel.astype(BF16))
        kx = placed if kx is None else kx + placed
    for p in range(n_pair):
        kx_ref[p] = kx[:, p * PAIR_W:(p + 1) * PAIR_W].astype(BF16)


def _fox_prep(logf, layer):
    _, b, t, nh = logf.shape
    return pl.pallas_call(
        _fox_prep_kernel, grid=(b,),
        in_specs=[pl.BlockSpec((None, None, t, nh), lambda i: (layer, i, 0, 0))],
        out_specs=pl.BlockSpec((None, nh // 2, t, PAIR_W), lambda i: (i, 0, 0, 0)),
        out_shape=jax.ShapeDtypeStruct((b, nh // 2, t, PAIR_W), BF16),
        compiler_params=_params(("arbitrary",)), name="fox_prep",
    )(logf)


def _moba_prep_kernel(q_ref, k_ref, qx_ref):
    t, w = q_ref.shape
    nh, nb = w // HEAD_DIM, t // MOBA_BLOCK
    kbar = jnp.mean(k_ref[...].reshape(nb, MOBA_BLOCK, w), axis=1)
    head_of_lane = lax.broadcasted_iota(jnp.int32, (nh, nb, w), 2) // HEAD_DIM
    head = lax.broadcasted_iota(jnp.int32, (nh, nb, w), 0)
    kb = jnp.where(head_of_lane == head, kbar[None], 0.0).reshape(nh * nb, w)
    s = _dot_nt(kb, q_ref[...], precision=lax.Precision.HIGHEST).reshape(nh, nb, t)
    own = lax.broadcasted_iota(jnp.int32, (nh, nb, t), 2) // MOBA_BLOCK
    n_idx = lax.broadcasted_iota(jnp.int32, (nh, nb, t), 1)
    past = n_idx < own
    s = jnp.where(past, s, -jnp.inf)
    rank = jnp.zeros((nh, nb, t), jnp.int32)
    for m in range(nb):
        sm = s[:, m:m + 1, :]
        rank = rank + ((sm > s) | ((sm == s) & (m < n_idx))).astype(jnp.int32)
    keep = (past & (rank < MOBA_TOPK)) | (n_idx >= own)
    bias = jnp.where(keep, 0.0, NEG).reshape(nh * nb, t)
    for p in range(nh // 2):
        rows = jnp.concatenate([bias[2 * p * nb:(2 * p + 2) * nb], jnp.zeros((PAIR_W - 2 * nb, t), F32)], axis=0)
        qx_ref[p] = rows.T.astype(BF16)


def _moba_prep(q, k, layer):
    b, t, w = q.shape
    n_pair = w // PAIR_W
    return pl.pallas_call(
        _moba_prep_kernel, grid=(b,),
        in_specs=[pl.BlockSpec((None, t, w), lambda i: (i, 0, 0)),
                  pl.BlockSpec((None, None, t, w), lambda i: (layer, i, 0, 0))],
        out_specs=pl.BlockSpec((None, n_pair, t, PAIR_W), lambda i: (i, 0, 0, 0)),
        out_shape=jax.ShapeDtypeStruct((b, n_pair, t, PAIR_W), BF16),
        compiler_params=_params(("arbitrary",)), name="moba_prep",
    )(q, k)


def _prompt_attn_kernel(q_ref, k_ref, v_ref, x_ref, o_ref, qs_ref, kp_ref, vb_ref, *, fox):
    t = q_ref.shape[0]
    nt = t // QTILE
    lane = lax.broadcasted_iota(jnp.int32, (t, PAIR_W), 1)
    row = lax.broadcasted_iota(jnp.int32, (t, PAIR_W), 0)
    first = lane < HEAD_DIM
    qs = q_ref[...] * ATTN_SCALE
    heads_q = (jnp.where(first, qs, 0.0).astype(BF16), jnp.where(first, 0.0, qs).astype(BF16))
    if fox:
        tile_lane = lax.broadcasted_iota(jnp.int32, (QTILE, PAIR_W), 1)
        ones_x = (jnp.where(tile_lane < 3, 1.0, 0.0).astype(BF16),
                  jnp.where((tile_lane >= 3) & (tile_lane < 6), 1.0, 0.0).astype(BF16))
        heads_x = None
        kx = x_ref[...]
    else:
        x = x_ref[...]
        zero = jnp.zeros_like(x)
        heads_x = (jnp.where(lane < nt, x, zero), jnp.where((lane >= nt) & (lane < 2 * nt), x, zero))
        kx = jnp.where((lane < 2 * nt) & (lane % nt == row // MOBA_BLOCK), 1.0, 0.0).astype(BF16)
    kp_ref[:, 0:PAIR_W] = k_ref[...].astype(BF16)
    kp_ref[:, PAIR_W:] = kx
    vb_ref[...] = v_ref[...].astype(BF16)
    for i in range(nt):
        for h in range(2):
            qs_ref[i, h * QTILE:(h + 1) * QTILE, 0:PAIR_W] = heads_q[h][i * QTILE:(i + 1) * QTILE]
            qs_ref[i, h * QTILE:(h + 1) * QTILE, PAIR_W:] = (
                ones_x[h] if fox else heads_x[h][i * QTILE:(i + 1) * QTILE])

    r = lax.broadcasted_iota(jnp.int32, (2 * QTILE, QTILE), 0) % QTILE
    c = lax.broadcasted_iota(jnp.int32, (2 * QTILE, QTILE), 1)
    causal = c <= r
    first_out = lax.broadcasted_iota(jnp.int32, (QTILE, PAIR_W), 1) < HEAD_DIM
    for i in range(nt):
        lo = i * QTILE
        qi = qs_ref[i]
        s_diag = jnp.where(causal, _dot_nt(qi, kp_ref[lo:lo + QTILE, :]), NEG)
        m = jnp.max(s_diag, axis=-1, keepdims=True)
        if i > 0:
            s_past = _dot_nt(qi, kp_ref[0:lo, :])
            m = jnp.maximum(m, jnp.max(s_past, axis=-1, keepdims=True))
        p_diag = jnp.exp(s_diag - m)
        l = jnp.sum(p_diag, axis=-1, keepdims=True)
        pv = _dot(p_diag.astype(BF16), vb_ref[lo:lo + QTILE, :])
        if i > 0:
            p_past = jnp.exp(s_past - m)
            l = l + jnp.sum(p_past, axis=-1, keepdims=True)
            pv = pv + _dot(p_past.astype(BF16), vb_ref[0:lo, :])
        o = pv / l
        o_ref[lo:lo + QTILE, :] = jnp.where(first_out, o[0:QTILE], o[QTILE:]).astype(o_ref.dtype)


def _prompt_attn(q, k, v, x, layer, fox):
    b, t, w = q.shape
    n_pair = w // PAIR_W
    kv_spec = pl.BlockSpec((None, None, t, PAIR_W), lambda i, p: (layer, i, 0, p))
    return pl.pallas_call(
        functools.partial(_prompt_attn_kernel, fox=fox), grid=(b, n_pair),
        in_specs=[pl.BlockSpec((None, t, PAIR_W), lambda i, p: (i, 0, p)), kv_spec, kv_spec,
                  pl.BlockSpec((None, None, t, PAIR_W), lambda i, p: (i, p, 0, 0))],
        out_specs=pl.BlockSpec((None, t, PAIR_W), lambda i, p: (i, 0, p)),
        out_shape=jax.ShapeDtypeStruct((b, t, w), BF16),
        scratch_shapes=[pltpu.VMEM((t // QTILE, 2 * QTILE, 2 * PAIR_W), BF16),
                        pltpu.VMEM((t, 2 * PAIR_W), BF16), pltpu.VMEM((t, PAIR_W), BF16)],
        compiler_params=_params(("arbitrary", "arbitrary")), name="fox_attn" if fox else "moba_attn",
    )(q, k, v, x)


def _page_cumsum_kernel(x_ref, o_ref, u_ref):
    n = x_ref.shape[1]

    @pl.when(pl.program_id(0) == 0)
    def _():
        src = lax.broadcasted_iota(jnp.int32, (n, n), 0)
        dst = lax.broadcasted_iota(jnp.int32, (n, n), 1)
        nh = n // PAGE_SIZE
        u_ref[...] = ((src % nh == dst % nh) & (src // nh <= dst // nh)).astype(BF16)

    hi, mid, lo = _split3(x_ref[...])
    u = u_ref[...]
    o_ref[...] = _dot(hi, u) + _dot(mid, u) + _dot(lo, u)


def _page_cumsum(logf_pages, tm):
    m, n = logf_pages.shape
    return pl.pallas_call(
        _page_cumsum_kernel, grid=(m // tm,),
        in_specs=[pl.BlockSpec((tm, n), lambda i: (i, 0))], out_specs=pl.BlockSpec((tm, n), lambda i: (i, 0)),
        out_shape=jax.ShapeDtypeStruct((m, n), F32), scratch_shapes=[pltpu.VMEM((n, n), BF16)],
        compiler_params=_params(("arbitrary",)), name="page_cumsum",
    )(logf_pages)


def _sample_attn_kernel(pt_ref, q_ref, kn_ref, vn_ref, *rest, fox, n_chunk):
    del pt_ref
    npg = PAGES_PER_STEP
    if fox:
        lfn_ref, rest = rest[0], rest[1:]
        pc_pages, rest = rest[2 * npg:3 * npg], rest[:2 * npg] + rest[3 * npg:]
    k_pages, v_pages = rest[:npg], rest[npg:2 * npg]
    o_ref, qb_ref, qf_ref, sc_ref, p_ref, sblk_ref, acc_ref, l_ref, carry_ref = rest[2 * npg:]
    tq, w = q_ref.shape
    nh = w // HEAD_DIM
    ncol = qb_ref.shape[0]
    blk_per_step = npg * PAGE_SIZE // MOBA_BLOCK
    n_blk = n_chunk * blk_per_step
    c = pl.program_id(1)

    col = lax.broadcasted_iota(jnp.int32, (nh, ncol), 1)
    expand = ((col // tq == lax.broadcasted_iota(jnp.int32, (nh, ncol), 0)) & (col < nh * tq)).astype(BF16)

    def head_bias(neg_c):
        hi, mid, lo = _split3(neg_c)
        return _dot(hi, expand) + _dot(mid, expand) + _dot(lo, expand)

    @pl.when(c == 0)
    def _():
        rows = lax.broadcasted_iota(jnp.int32, (ncol, w), 0)
        lanes = lax.broadcasted_iota(jnp.int32, (ncol, w), 1)
        qrep = jnp.concatenate([q_ref[...]] * (ncol // tq), axis=0)
        qbd = jnp.where((rows // tq == lanes // HEAD_DIM) & (rows < nh * tq), qrep, 0.0)
        qf_ref[...] = qbd
        qb_ref[...] = (qbd * ATTN_SCALE).astype(BF16)
        acc_ref[...] = jnp.zeros_like(acc_ref)
        carry_ref[...] = jnp.zeros_like(carry_ref)

    @pl.when(c < n_chunk)
    def _():
        qb = qb_ref[...]
        carry = carry_ref[...]
        kbars = []
        for j in range(blk_per_step):
            kf = jnp.concatenate([k_pages[2 * j][...], k_pages[2 * j + 1][...]], axis=0)
            s = _dot_nt(kf.astype(BF16), qb)
            if fox:
                cs0 = pc_pages[2 * j][...] + carry
                cs1 = pc_pages[2 * j + 1][...] + cs0[PAGE_SIZE - 1:PAGE_SIZE, :]
                carry = cs1[PAGE_SIZE - 1:PAGE_SIZE, :]
                s = s + head_bias(-jnp.concatenate([cs0, cs1], axis=0))
            else:
                kbars.append(jnp.mean(kf, axis=0, keepdims=True))
            off = pl.multiple_of((c * blk_per_step + j) * MOBA_BLOCK, MOBA_BLOCK)
            sc_ref[pl.ds(off, MOBA_BLOCK), :] = s
        if fox:
            carry_ref[...] = carry
        else:
            kbar = jnp.concatenate(kbars, axis=0)
            sblk = _dot_nt(kbar, qf_ref[...], precision=lax.Precision.HIGHEST)
            sblk_ref[pl.ds(pl.multiple_of(c * blk_per_step, blk_per_step), blk_per_step), :] = sblk

    @pl.when(c == n_chunk - 1)
    def _():
        s_new = _dot_nt(kn_ref[...].astype(BF16), qb_ref[...])
        if fox:
            cs = carry_ref[...]
            rows = []
            for i in range(tq):
                cs = cs + lfn_ref[i:i + 1, :]
                rows.append(cs)
            s_new = s_new + head_bias(-jnp.concatenate(rows, axis=0))
        key = lax.broadcasted_iota(jnp.int32, (tq, ncol), 0)
        qry = lax.broadcasted_iota(jnp.int32, (tq, ncol), 1) % tq
        tail0 = n_blk * MOBA_BLOCK
        sc_ref[tail0:tail0 + tq, :] = jnp.where(key <= qry, s_new, NEG)
        sc_ref[tail0 + tq:, :] = jnp.full((MOBA_BLOCK - tq, ncol), NEG, F32)

        if fox:
            blk_bias = None
        else:
            sb = sblk_ref[...]
            n_idx = lax.broadcasted_iota(jnp.int32, sb.shape, 0)
            rank = jnp.zeros(sb.shape, jnp.int32)
            for m_i in range(n_blk):
                sm = sb[m_i:m_i + 1, :]
                rank = rank + ((sm > sb) | ((sm == sb) & (m_i < n_idx))).astype(jnp.int32)
            blk_bias = jnp.where(rank < MOBA_TOPK, 0.0, NEG)

        def block(n):
            s = sc_ref[n * MOBA_BLOCK:(n + 1) * MOBA_BLOCK, :]
            if blk_bias is not None and n < n_blk:
                s = s + blk_bias[n:n + 1, :]
            return s

        m = None
        for n in range(n_blk + 1):
            bm = jnp.max(block(n), axis=0, keepdims=True)
            m = bm if m is None else jnp.maximum(m, bm)
        l = jnp.zeros_like(m)
        for n in range(n_blk + 1):
            p = jnp.exp(block(n) - m)
            l = l + jnp.sum(p, axis=0, keepdims=True)
            p_ref[n * MOBA_BLOCK:(n + 1) * MOBA_BLOCK, :] = p.astype(BF16)
        l_ref[...] = l

    @pl.when(c >= n_chunk)
    def _():
        acc = acc_ref[...]
        for j in range(blk_per_step):
            vb = jnp.concatenate([v_pages[2 * j][...], v_pages[2 * j + 1][...]], axis=0).astype(BF16)
            off = pl.multiple_of(((c - n_chunk) * blk_per_step + j) * MOBA_BLOCK, MOBA_BLOCK)
            acc = acc + _dot_tn(p_ref[pl.ds(off, MOBA_BLOCK), :], vb)
        acc_ref[...] = acc

    @pl.when(c == 2 * n_chunk - 1)
    def _():
        tail0 = n_blk * MOBA_BLOCK
        pad = 16 - tq
        v_new = jnp.concatenate([vn_ref[...], jnp.zeros((pad, w), F32)], axis=0).astype(BF16)
        acc = acc_ref[...] + _dot_tn(p_ref[tail0:tail0 + tq + pad, :], v_new)
        l_col = jnp.broadcast_to(l_ref[...], (ncol, ncol)).T
        o_full = acc / jnp.concatenate([l_col] * (w // ncol), axis=1)
        lane_head = lax.broadcasted_iota(jnp.int32, (tq, w), 1) // HEAD_DIM
        out = jnp.zeros((tq, w), F32)
        for h in range(nh):
            out = out + jnp.where(lane_head == h, o_full[h * tq:(h + 1) * tq, :], 0.0)
        o_ref[...] = out.astype(o_ref.dtype)


def _sample_attn(page_table, q, k_new, v_new, cache_k, cache_v, layer, lf_new=None, page_cs=None):
    fox = lf_new is not None
    b, tq, w = q.shape
    nh = w // HEAD_DIM
    npg = PAGES_PER_STEP
    n_pages = page_table.shape[0] // b
    n_chunk = n_pages // npg
    n_blk = n_pages * PAGE_SIZE // MOBA_BLOCK
    ncol = 128

    def k_page(r):
        return lambda i, c, pt: (layer, pt[i * n_pages + jnp.minimum(c, n_chunk - 1) * npg + r], 0, 0)

    def v_page(r):
        def index(i, c, pt):
            seq = jnp.where(c >= n_chunk, i, jnp.maximum(i - 1, 0))
            chunk = jnp.where(c >= n_chunk, c - n_chunk, jnp.where(i > 0, n_chunk - 1, 0))
            return (layer, pt[seq * n_pages + chunk * npg + r], 0, 0)
        return index

    new_spec = pl.BlockSpec((None, None, tq, w), lambda i, c, pt: (layer, i, 0, 0))
    in_specs = [pl.BlockSpec((None, tq, w), lambda i, c, pt: (i, 0, 0)), new_spec, new_spec]
    args = [q, k_new, v_new]
    if fox:
        in_specs.append(pl.BlockSpec((None, None, tq, nh), lambda i, c, pt: (layer, i, 0, 0)))
        args.append(lf_new)
    in_specs += [pl.BlockSpec((None, None, PAGE_SIZE, w), k_page(r)) for r in range(npg)]
    in_specs += [pl.BlockSpec((None, None, PAGE_SIZE, w), v_page(r)) for r in range(npg)]
    args += [cache_k] * npg + [cache_v] * npg
    if fox:
        in_specs += [pl.BlockSpec((None, None, PAGE_SIZE, nh), k_page(r)) for r in range(npg)]
        args += [page_cs] * npg
    rows = (n_blk + 1) * MOBA_BLOCK
    grid_spec = pltpu.PrefetchScalarGridSpec(
        num_scalar_prefetch=1, grid=(b, 2 * n_chunk), in_specs=in_specs,
        out_specs=pl.BlockSpec((None, tq, w), lambda i, c, pt: (i, 0, 0)),
        scratch_shapes=[pltpu.VMEM((ncol, w), BF16), pltpu.VMEM((ncol, w), F32), pltpu.VMEM((rows, ncol), F32),
                        pltpu.VMEM((rows, ncol), BF16), pltpu.VMEM((n_blk, ncol), F32), pltpu.VMEM((ncol, w), F32),
                        pltpu.VMEM((1, ncol), F32), pltpu.VMEM((1, nh), F32)])
    return pl.pallas_call(
        functools.partial(_sample_attn_kernel, fox=fox, n_chunk=n_chunk), grid_spec=grid_spec,
        out_shape=jax.ShapeDtypeStruct((b, tq, w), BF16),
        compiler_params=_params(("arbitrary", "arbitrary")), name="fox_sample" if fox else "moba_sample",
    )(page_table, *args)


def _group_layer(x, layer, depth, wts, stacked, mixers, tm, final_norm):
    qa, u, qc, stacked = _project(x, wts["norm1"], wts["w_main"], wts["w_f"], wts["b_f"], stacked, layer, depth, tm)
    a, b, c = mixers(qa, u, qc, stacked)
    x = _merge(x, wts["norm1"], a, b, c, wts["w_g"], wts["w_a"], wts["w_b"], wts["w_c"], wts["w_o"], tm)
    x = _ffn(x, wts["norm2"], wts["w_fi"], wts["w_fo"], wts["norm_f"], final_norm, tm)
    return x, u, stacked


def kernel(x_prompt, x_sample, cache_a_k, cache_a_v, cache_c_k, cache_c_v, cache_c_logf, state_pool, page_table,
           norm1, w_in, b_f, w_pool, pool_scale, w_br_a, w_br_b, w_br_c, w_out, norm2, w_ffn_in, w_ffn_out, norm_f):
    bp, tp, d = x_prompt.shape
    bs, ts, _ = x_sample.shape
    depth, n_phys, page, nh, hd = cache_a_k.shape
    w = nh * hd
    n_state = state_pool.shape[2]
    past_len = page_table.shape[1] * page
    n_main = 7 * w

    paged = lambda cache: cache.reshape(depth, n_phys, page, w)
    cak, cav, cck, ccv = paged(cache_a_k), paged(cache_a_v), paged(cache_c_k), paged(cache_c_v)
    page_cs = _page_cumsum(cache_c_logf.reshape(depth * n_phys, page * nh), 1024).reshape(depth, n_phys, page, nh)
    pt_flat = page_table.reshape(-1)
    hist_p = jnp.zeros((bp, POOL_HIST, w), F32)
    hist_s = jnp.pad(state_pool, ((0, 0), (0, 0), (POOL_HIST - n_state, 0), (0, 0)))

    xp, xs = x_prompt.reshape(bp * tp, d), x_sample.reshape(bs * ts, d)
    st_p = st_s = None
    pool_p, pool_s = [], []
    for l in range(depth):
        wts = dict(norm1=norm1[l][None], norm2=norm2[l][None], norm_f=norm_f[None], b_f=b_f[l][None],
                   w_main=w_in[l, :, :n_main].astype(BF16), w_f=w_in[l, :, n_main:n_main + nh].astype(BF16),
                   w_g=w_in[l, :, n_main + nh:].astype(BF16), w_a=w_br_a[l].astype(BF16),
                   w_b=w_br_b[l].astype(BF16), w_c=w_br_c[l].astype(BF16), w_o=w_out[l].astype(BF16),
                   w_fi=w_ffn_in[l].astype(BF16), w_fo=w_ffn_out[l].astype(BF16))
        wp, scale = w_pool[l].astype(BF16), pool_scale[l][None]
        last = l == depth - 1

        def prompt_mixers(qa, u, qc, st):
            ka, va, kc, vc, lf = (s.reshape(depth, bp, tp, -1) for s in st)
            qa3, qc3 = qa.reshape(bp, tp, w), qc.reshape(bp, tp, w)
            a = _prompt_attn(qa3, ka, va, _moba_prep(qa3, ka, l), l, fox=False)
            b = _pool(u.reshape(bp, tp, w), hist_p, wp, scale, 0, 1)
            c = _prompt_attn(qc3, kc, vc, _fox_prep(lf, l), l, fox=True)
            return a.reshape(bp * tp, w), b.reshape(bp * tp, w), c.reshape(bp * tp, w)

        def sample_mixers(qa, u, qc, st):
            ka, va, kc, vc, lf = (s.reshape(depth, bs, ts, -1) for s in st)
            a = _sample_attn(pt_flat, qa.reshape(bs, ts, w), ka, va, cak, cav, l)
            b = _pool(u.reshape(bs, ts, w), hist_s[l], wp, scale, past_len, bs)
            c = _sample_attn(pt_flat, qc.reshape(bs, ts, w), kc, vc, cck, ccv, l, lf_new=lf, page_cs=page_cs)
            return a.reshape(bs * ts, w), b.reshape(bs * ts, w), c.reshape(bs * ts, w)

        xp, u_p, st_p = _group_layer(xp, l, depth, wts, st_p, prompt_mixers, 512, last)
        xs, u_s, st_s = _group_layer(xs, l, depth, wts, st_s, sample_mixers, bs * ts, last)
        pool_p.append(u_p.reshape(bp, tp, w)[:, tp - n_state:])
        pool_s.append(jnp.concatenate([state_pool[l], u_s.reshape(bs, ts, w)], axis=1)[:, ts:])

    heads_p = lambda s: s.reshape(depth, bp, tp, nh, hd)
    heads_s = lambda s: s.reshape(depth, bs, ts, nh, hd)
    return (xp.reshape(bp, tp, d), xs.reshape(bs, ts, d),
            heads_p(st_p[0]), heads_p(st_p[1]), heads_p(st_p[2]), heads_p(st_p[3]),
            st_p[4].reshape(depth, bp, tp, nh), jnp.stack(pool_p),
            heads_s(st_s[0]), heads_s(st_s[1]), heads_s(st_s[2]), heads_s(st_s[3]),
            st_s[4].reshape(depth, bs, ts, nh), jnp.stack(pool_s))
```

```python
import functools

import jax
import jax.numpy as jnp
from jax import lax
from jax.experimental import pallas as pl
from jax.experimental.pallas import tpu as pltpu

F32 = jnp.float32
BF16 = jnp.bfloat16

HEAD_DIM = 64
PAIR_W = 2 * HEAD_DIM
MOBA_BLOCK = 256
MOBA_TOPK = 3
POOL_WINDOWS = (2, 4, 8, 16)
POOL_GROUP_W = 128
POOL_HIST = max(POOL_WINDOWS)
RMS_EPS = 1e-6
ATTN_SCALE = HEAD_DIM ** -0.5
NEG = -1e30
QTILE = 256
PAGES_PER_STEP = 32
BF16_ROWS = 16
SCORE_COLS = 128
V7X_VMEM_LIMIT = 56 * 1024 * 1024


def _params(sem):
    return pltpu.CompilerParams(dimension_semantics=sem, vmem_limit_bytes=V7X_VMEM_LIMIT)


def _rmsnorm(x, g):
    ms = jnp.mean(x * x, axis=-1, keepdims=True)
    return x * lax.rsqrt(ms + RMS_EPS) * g


def _log_sigmoid(x):
    return -(jnp.maximum(-x, 0.0) + jnp.log1p(jnp.exp(-jnp.abs(x))))


def _dot(a, b):
    return jnp.dot(a, b, preferred_element_type=F32)


def _dot_nt(a, b, precision=None):
    return lax.dot_general(a, b, (((1,), (1,)), ((), ())), preferred_element_type=F32, precision=precision)


def _split3(x):
    hi = x.astype(BF16)
    r = x - hi.astype(F32)
    mid = r.astype(BF16)
    lo = (r - mid.astype(F32)).astype(BF16)
    return hi, mid, lo


def _dot3(a, x):
    hi, mid, lo = _split3(x)
    return _dot(a, hi) + _dot(a, mid) + _dot(a, lo)


def _pad_rows(x, rows):
    return jnp.concatenate([x, jnp.zeros((rows - x.shape[0],) + x.shape[1:], x.dtype)], axis=0)


def _upper_tri(n):
    r = lax.broadcasted_iota(jnp.int32, (n, n), 0)
    c = lax.broadcasted_iota(jnp.int32, (n, n), 1)
    return (r <= c).astype(BF16)


def _proj_kernel(*refs, n_alias, transposed):
    x_ref, g_ref, w_ref, wf_ref, bf_ref = refs[:5]
    qa_ref, ka_ref, va_ref, u_ref, qc_ref, kc_ref, vc_ref, lf_ref = refs[5 + n_alias:]
    w = qa_ref.shape[-1]
    nh = w // HEAD_DIM
    hb = _rmsnorm(x_ref[...], g_ref[...]).astype(BF16)
    for j, o_ref in ((0, qa_ref), (3, u_ref), (4, qc_ref)):
        o_ref[...] = _dot_nt(hb, w_ref[j * w:(j + 1) * w, :])
    if transposed:
        for j, k_ref, v_ref in ((1, ka_ref, va_ref), (5, kc_ref, vc_ref)):
            kv = _dot_nt(w_ref[j * w:(j + 2) * w, :], hb)
            k_ref[...] = kv[0:w]
            v_ref[...] = kv[w:2 * w]
        lf_ref[...] = _log_sigmoid(_dot_nt(wf_ref[...], hb)[0:nh] + bf_ref[...])
    else:
        for j, o_ref in ((1, ka_ref), (2, va_ref), (5, kc_ref), (6, vc_ref)):
            o_ref[...] = _dot_nt(hb, w_ref[j * w:(j + 1) * w, :])
        lf_ref[...] = _log_sigmoid(_dot_nt(hb, wf_ref[...])[:, 0:nh] + bf_ref[...])


def _project(x, g, w_t, wf_t, b_f, stacked, layer, depth, tm, seq_len=None):
    m, d = x.shape
    w = w_t.shape[0] // 7
    nh = w // HEAD_DIM
    transposed = seq_len is not None
    row = lambda i: (i, 0)
    const = lambda i: (0, 0)
    in_specs = [pl.BlockSpec((tm, d), row), pl.BlockSpec((1, d), const), pl.BlockSpec(w_t.shape, const),
                pl.BlockSpec(wf_t.shape, const), pl.BlockSpec(b_f.shape, const)]
    args = [x, g, w_t, wf_t, b_f]
    aliases = {}
    if stacked is not None:
        in_specs += [pl.BlockSpec(memory_space=pl.ANY)] * 5
        args += list(stacked)
        aliases = {5: 1, 6: 2, 7: 5, 8: 6, 9: 7}
    plain = pl.BlockSpec((tm, w), row)
    plain_s = jax.ShapeDtypeStruct((m, w), F32)
    if transposed:
        nt = seq_len // tm
        lay = lambda i: (layer, i // nt, 0, i % nt)
        stack, lf_spec = pl.BlockSpec((None, None, w, tm), lay), pl.BlockSpec((None, None, nh, tm), lay)
        stack_s = jax.ShapeDtypeStruct((depth, m // seq_len, w, seq_len), F32)
        lf_s = jax.ShapeDtypeStruct((depth, m // seq_len, nh, seq_len), F32)
    else:
        lay = lambda i: (layer, i, 0)
        stack, lf_spec = pl.BlockSpec((None, tm, w), lay), pl.BlockSpec((None, tm, nh), lay)
        stack_s = jax.ShapeDtypeStruct((depth, m, w), F32)
        lf_s = jax.ShapeDtypeStruct((depth, m, nh), F32)
    qa, ka, va, u, qc, kc, vc, lf = pl.pallas_call(
        functools.partial(_proj_kernel, n_alias=len(aliases), transposed=transposed), grid=(m // tm,),
        in_specs=in_specs, out_specs=[plain, stack, stack, plain, plain, stack, stack, lf_spec],
        out_shape=[plain_s, stack_s, stack_s, plain_s, plain_s, stack_s, stack_s, lf_s],
        input_output_aliases=aliases, compiler_params=_params(("arbitrary",)), name="proj",
    )(*args)
    return qa, u, qc, (ka, va, kc, vc, lf)


def _merge_kernel(x_ref, g_ref, a_ref, b_ref, c_ref, wg_ref, wa_ref, wb_ref, wc_ref, wo_ref, o_ref):
    x = x_ref[...]
    d = x.shape[-1]
    hb = _rmsnorm(x, g_ref[...]).astype(BF16)
    y = None
    for i, (br_ref, w_ref) in enumerate(((a_ref, wa_ref), (b_ref, wb_ref), (c_ref, wc_ref))):
        gate = jax.nn.sigmoid(_dot_nt(hb, wg_ref[i * d:(i + 1) * d, :]))
        t = gate * _dot(br_ref[...], w_ref[...])
        y = t if y is None else y + t
    o_ref[...] = x + _dot(y.astype(BF16), wo_ref[...])


def _merge(x, g, a, b, c, wg_t, w_a, w_b, w_c, w_o, tm):
    m, d = x.shape
    row = lambda i: (i, 0)
    const = lambda i: (0, 0)
    full = lambda arr: pl.BlockSpec(arr.shape, const)
    br = lambda arr: pl.BlockSpec((tm, arr.shape[1]), row)
    return pl.pallas_call(
        _merge_kernel, grid=(m // tm,),
        in_specs=[pl.BlockSpec((tm, d), row), pl.BlockSpec((1, d), const), br(a), br(b), br(c),
                  full(wg_t), full(w_a), full(w_b), full(w_c), full(w_o)],
        out_specs=pl.BlockSpec((tm, d), row), out_shape=jax.ShapeDtypeStruct((m, d), F32),
        compiler_params=_params(("arbitrary",)), name="merge",
    )(x, g, a, b, c, wg_t, w_a, w_b, w_c, w_o)


def _ffn_kernel(x_ref, g_ref, wi_ref, wo_ref, gf_ref, o_ref, *, chunk, final_norm):
    x = x_ref[...]
    d_ff = wo_ref.shape[0]
    hb = _rmsnorm(x, g_ref[...]).astype(BF16)
    acc = x
    for c0 in range(0, d_ff, chunk):
        gt = _dot(hb, wi_ref[:, c0:c0 + chunk])
        up = _dot(hb, wi_ref[:, d_ff + c0:d_ff + c0 + chunk])
        act = (gt * jax.nn.sigmoid(gt) * up).astype(BF16)
        acc = acc + _dot(act, wo_ref[c0:c0 + chunk, :])
    o_ref[...] = _rmsnorm(acc, gf_ref[...]) if final_norm else acc


def _ffn(x, g, w_i, w_o, g_final, final_norm, tm, chunk=256):
    m, d = x.shape
    row = lambda i: (i, 0)
    const = lambda i: (0, 0)
    return pl.pallas_call(
        functools.partial(_ffn_kernel, chunk=chunk, final_norm=final_norm), grid=(m // tm,),
        in_specs=[pl.BlockSpec((tm, d), row), pl.BlockSpec((1, d), const), pl.BlockSpec(w_i.shape, const),
                  pl.BlockSpec(w_o.shape, const), pl.BlockSpec((1, d), const)],
        out_specs=pl.BlockSpec((tm, d), row), out_shape=jax.ShapeDtypeStruct((m, d), F32),
        compiler_params=_params(("arbitrary",)), name="ffn",
    )(x, g, w_i, w_o, g_final)


def _pool_kernel(u_ref, hist_ref, w_ref, scale_ref, o_ref, s_ref, *, pos0):
    nb, t, _ = u_ref.shape
    ext = t + POOL_HIST
    pos = pos0 + lax.broadcasted_iota(jnp.int32, (1, t, 1), 1)
    for gi, win in enumerate(POOL_WINDOWS):
        sl = slice(gi * POOL_GROUP_W, (gi + 1) * POOL_GROUP_W)
        u = u_ref[:, :, sl]
        s_ref[:, 0:POOL_HIST, :] = hist_ref[:, :, sl]
        s_ref[:, POOL_HIST:, :] = u
        shift = 1
        while shift < win:
            s_ref[:, shift:, :] = s_ref[:, shift:, :] + s_ref[:, 0:ext - shift, :]
            shift *= 2
        cnt = jnp.minimum(win, pos + 1).astype(F32)
        dlt = s_ref[:, POOL_HIST:, :] / cnt - u
        y = _dot(dlt.reshape(nb * t, POOL_GROUP_W).astype(BF16), w_ref[gi])
        o_ref[:, :, sl] = (y.reshape(nb, t, POOL_GROUP_W) * scale_ref[:, sl]).astype(o_ref.dtype)


def _pool(u, hist, w_pool, scale, pos0, nb):
    b, t, w = u.shape
    blk = lambda i: (i, 0, 0)
    return pl.pallas_call(
        functools.partial(_pool_kernel, pos0=pos0), grid=(b // nb,),
        in_specs=[pl.BlockSpec((nb, t, w), blk), pl.BlockSpec((nb, POOL_HIST, w), blk),
                  pl.BlockSpec(w_pool.shape, lambda i: (0, 0, 0)), pl.BlockSpec((1, w), lambda i: (0, 0))],
        out_specs=pl.BlockSpec((nb, t, w), blk), out_shape=jax.ShapeDtypeStruct((b, t, w), BF16),
        scratch_shapes=[pltpu.VMEM((nb, t + POOL_HIST, POOL_GROUP_W), F32)],
        compiler_params=_params(("arbitrary",)), name="pool",
    )(u, hist, w_pool, scale)


def _fox_prep_kernel(lf_ref, kx_ref):
    nh, t = lf_ref.shape
    tri = _upper_tri(QTILE)
    carry = jnp.zeros((BF16_ROWS, 1), F32)
    parts = []
    for blk in range(t // QTILE):
        x = _pad_rows(lf_ref[:, blk * QTILE:(blk + 1) * QTILE], BF16_ROWS)
        hi, mid, lo = _split3(x)
        cs = _dot(hi, tri) + _dot(mid, tri) + _dot(lo, tri) + carry
        carry = cs[:, QTILE - 1:QTILE]
        parts.append(cs)
    neg_c = -jnp.concatenate(parts, axis=1)
    terms = [term.astype(F32) for term in _split3(neg_c)]
    row = lax.broadcasted_iota(jnp.int32, (BF16_ROWS, t), 0)
    for p in range(nh // 2):
        out = jnp.zeros((BF16_ROWS, t), F32)
        for head in range(2):
            for part in range(3):
                src = terms[part][2 * p + head:2 * p + head + 1, :]
                out = jnp.where(row == 3 * head + part, src, out)
        kx_ref[p] = out.astype(BF16)


def _fox_prep(logf_t, layer):
    _, b, nh, t = logf_t.shape
    return pl.pallas_call(
        _fox_prep_kernel, grid=(b,),
        in_specs=[pl.BlockSpec((None, None, nh, t), lambda i: (layer, i, 0, 0))],
        out_specs=pl.BlockSpec((None, nh // 2, BF16_ROWS, t), lambda i: (i, 0, 0, 0)),
        out_shape=jax.ShapeDtypeStruct((b, nh // 2, BF16_ROWS, t), BF16),
        compiler_params=_params(("arbitrary",)), name="fox_prep",
    )(logf_t)


def _moba_prep_kernel(q_ref, kt_ref, qx_ref):
    t, w = q_ref.shape
    nh, nb = w // HEAD_DIM, t // MOBA_BLOCK
    lane = lax.broadcasted_iota(jnp.int32, (w, PAIR_W), 1)
    kbar_t = jnp.zeros((w, PAIR_W), F32)
    for n in range(nb):
        col = jnp.mean(kt_ref[:, n * MOBA_BLOCK:(n + 1) * MOBA_BLOCK], axis=1, keepdims=True)
        kbar_t = jnp.where(lane == n, col, kbar_t)
    kbar = kbar_t.T[0:nb]
    head_of_lane = lax.broadcasted_iota(jnp.int32, (nh, nb, w), 2) // HEAD_DIM
    head = lax.broadcasted_iota(jnp.int32, (nh, nb, w), 0)
    kb = jnp.where(head_of_lane == head, kbar[None], 0.0).reshape(nh * nb, w)
    s = _dot_nt(kb, q_ref[...], precision=lax.Precision.HIGHEST).reshape(nh, nb, t)
    own = lax.broadcasted_iota(jnp.int32, (nh, nb, t), 2) // MOBA_BLOCK
    n_idx = lax.broadcasted_iota(jnp.int32, (nh, nb, t), 1)
    past = n_idx < own
    s = jnp.where(past, s, -jnp.inf)
    rank = jnp.zeros((nh, nb, t), jnp.int32)
    for m in range(nb):
        sm = s[:, m:m + 1, :]
        rank = rank + ((sm > s) | ((sm == s) & (m < n_idx))).astype(jnp.int32)
    keep = (past & (rank < MOBA_TOPK)) | (n_idx >= own)
    bias = jnp.where(keep, 0.0, NEG).reshape(nh * nb, t)
    for p in range(nh // 2):
        qx_ref[p] = _pad_rows(bias[2 * p * nb:(2 * p + 2) * nb], PAIR_W).T.astype(BF16)


def _moba_prep(q, k_t, layer):
    b, t, w = q.shape
    n_pair = w // PAIR_W
    return pl.pallas_call(
        _moba_prep_kernel, grid=(b,),
        in_specs=[pl.BlockSpec((None, t, w), lambda i: (i, 0, 0)),
                  pl.BlockSpec((None, None, w, t), lambda i: (layer, i, 0, 0))],
        out_specs=pl.BlockSpec((None, n_pair, t, PAIR_W), lambda i: (i, 0, 0, 0)),
        out_shape=jax.ShapeDtypeStruct((b, n_pair, t, PAIR_W), BF16),
        compiler_params=_params(("arbitrary",)), name="moba_prep",
    )(q, k_t)


def _prompt_attn_kernel(q_ref, kt_ref, vt_ref, x_ref, o_ref, qs_ref, kp_ref, vb_ref, *, fox):
    t = q_ref.shape[0]
    nt = t // QTILE
    lane = lax.broadcasted_iota(jnp.int32, (t, PAIR_W), 1)
    first = lane < HEAD_DIM
    qs = q_ref[...] * ATTN_SCALE
    heads_q = (jnp.where(first, qs, 0.0).astype(BF16), jnp.where(first, 0.0, qs).astype(BF16))
    kp_ref[0:PAIR_W, :] = kt_ref[...].astype(BF16)
    if fox:
        tile_lane = lax.broadcasted_iota(jnp.int32, (QTILE, PAIR_W), 1)
        ones_x = (jnp.where(tile_lane < 3, 1.0, 0.0).astype(BF16),
                  jnp.where((tile_lane >= 3) & (tile_lane < 6), 1.0, 0.0).astype(BF16))
        heads_x = None
        kp_ref[PAIR_W:PAIR_W + BF16_ROWS, :] = x_ref[...]
        kp_ref[PAIR_W + BF16_ROWS:, :] = jnp.zeros((PAIR_W - BF16_ROWS, t), BF16)
    else:
        x = x_ref[...]
        zero = jnp.zeros_like(x)
        heads_x = (jnp.where(lane < nt, x, zero), jnp.where((lane >= nt) & (lane < 2 * nt), x, zero))
        xrow = lax.broadcasted_iota(jnp.int32, (PAIR_W, t), 0)
        xcol = lax.broadcasted_iota(jnp.int32, (PAIR_W, t), 1)
        kp_ref[PAIR_W:, :] = jnp.where((xrow < 2 * nt) & (xrow % nt == xcol // MOBA_BLOCK), 1.0, 0.0).astype(BF16)
    vb_ref[...] = vt_ref[...].astype(BF16)
    for i in range(nt):
        for h in range(2):
            qs_ref[i, h * QTILE:(h + 1) * QTILE, 0:PAIR_W] = heads_q[h][i * QTILE:(i + 1) * QTILE]
            qs_ref[i, h * QTILE:(h + 1) * QTILE, PAIR_W:] = (
                ones_x[h] if fox else heads_x[h][i * QTILE:(i + 1) * QTILE])

    r = lax.broadcasted_iota(jnp.int32, (2 * QTILE, QTILE), 0) % QTILE
    c = lax.broadcasted_iota(jnp.int32, (2 * QTILE, QTILE), 1)
    causal = c <= r
    first_out = lax.broadcasted_iota(jnp.int32, (QTILE, PAIR_W), 1) < HEAD_DIM
    for i in range(nt):
        lo = i * QTILE
        qi = qs_ref[i]
        s_diag = jnp.where(causal, _dot(qi, kp_ref[:, lo:lo + QTILE]), NEG)
        m = jnp.max(s_diag, axis=-1, keepdims=True)
        if i > 0:
            s_past = _dot(qi, kp_ref[:, 0:lo])
            m = jnp.maximum(m, jnp.max(s_past, axis=-1, keepdims=True))
        p_diag = jnp.exp(s_diag - m)
        l = jnp.sum(p_diag, axis=-1, keepdims=True)
        pv = _dot_nt(p_diag.astype(BF16), vb_ref[:, lo:lo + QTILE])
        if i > 0:
            p_past = jnp.exp(s_past - m)
            l = l + jnp.sum(p_past, axis=-1, keepdims=True)
            pv = pv + _dot_nt(p_past.astype(BF16), vb_ref[:, 0:lo])
        o = pv / l
        o_ref[lo:lo + QTILE, :] = jnp.where(first_out, o[0:QTILE], o[QTILE:]).astype(o_ref.dtype)


def _prompt_attn(q, k_t, v_t, x, layer, fox):
    b, t, w = q.shape
    n_pair = w // PAIR_W
    kv_spec = pl.BlockSpec((None, None, PAIR_W, t), lambda i, p: (layer, i, p, 0))
    x_spec = pl.BlockSpec((None, None) + x.shape[2:], lambda i, p: (i, p, 0, 0))
    return pl.pallas_call(
        functools.partial(_prompt_attn_kernel, fox=fox), grid=(b, n_pair),
        in_specs=[pl.BlockSpec((None, t, PAIR_W), lambda i, p: (i, 0, p)), kv_spec, kv_spec, x_spec],
        out_specs=pl.BlockSpec((None, t, PAIR_W), lambda i, p: (i, 0, p)),
        out_shape=jax.ShapeDtypeStruct((b, t, w), BF16),
        scratch_shapes=[pltpu.VMEM((t // QTILE, 2 * QTILE, 2 * PAIR_W), BF16),
                        pltpu.VMEM((2 * PAIR_W, t), BF16), pltpu.VMEM((PAIR_W, t), BF16)],
        compiler_params=_params(("arbitrary", "arbitrary")), name="fox_attn" if fox else "moba_attn",
    )(q, k_t, v_t, x)


def _page_cumsum_kernel(x_ref, o_ref):
    o_ref[...] = _dot3_right(x_ref[...], _upper_tri(x_ref.shape[1]))


def _dot3_right(x, a):
    hi, mid, lo = _split3(x)
    return _dot(hi, a) + _dot(mid, a) + _dot(lo, a)


def _page_cumsum(logf_rows, tm):
    m, n = logf_rows.shape
    return pl.pallas_call(
        _page_cumsum_kernel, grid=(m // tm,),
        in_specs=[pl.BlockSpec((tm, n), lambda i: (i, 0))], out_specs=pl.BlockSpec((tm, n), lambda i: (i, 0)),
        out_shape=jax.ShapeDtypeStruct((m, n), F32),
        compiler_params=_params(("arbitrary",)), name="page_cumsum",
    )(logf_rows)


def _sample_attn_kernel(pt_ref, q_ref, kn_ref, vn_ref, *rest, fox, n_chunk):
    del pt_ref
    npg = PAGES_PER_STEP
    if fox:
        lfn_ref, rest = rest[0], rest[1:]
        pc_pages, rest = rest[2 * npg:3 * npg], rest[:2 * npg] + rest[3 * npg:]
    k_pages, v_pages = rest[:npg], rest[npg:2 * npg]
    o_ref, qb_ref, qf_ref, sc_ref, p_ref, sblk_ref, acc_ref, l_ref, carry_ref, ctot_ref = rest[2 * npg:]
    tq, w = q_ref.shape
    nh = w // HEAD_DIM
    nrow = qb_ref.shape[0]
    page = k_pages[0].shape[1]
    pg_per_blk = MOBA_BLOCK // page
    blk_per_step = npg // pg_per_blk
    n_blk = n_chunk * blk_per_step
    c = pl.program_id(1)

    erow = lax.broadcasted_iota(jnp.int32, (nrow, BF16_ROWS), 0)
    ecol = lax.broadcasted_iota(jnp.int32, (nrow, BF16_ROWS), 1)
    in_head = (erow // tq == ecol) & (erow < nh * tq)
    expand = in_head.astype(BF16)

    @pl.when(c == 0)
    def _():
        rows = lax.broadcasted_iota(jnp.int32, (nrow, w), 0)
        lanes = lax.broadcasted_iota(jnp.int32, (nrow, w), 1)
        qrep = jnp.concatenate([q_ref[...]] * (nrow // tq), axis=0)
        qbd = jnp.where((rows // tq == lanes // HEAD_DIM) & (rows < nh * tq), qrep, 0.0)
        qf_ref[...] = qbd.T
        qb_ref[...] = (qbd * ATTN_SCALE).astype(BF16)
        acc_ref[...] = jnp.zeros_like(acc_ref)
        carry_ref[...] = jnp.zeros_like(carry_ref)

    @pl.when(c < n_chunk)
    def _():
        qb = qb_ref[...]
        carry = carry_ref[...]
        for j in range(blk_per_step):
            pages = range(j * pg_per_blk, (j + 1) * pg_per_blk)
            kt = jnp.concatenate([k_pages[r][...] for r in pages], axis=1)
            s = _dot(qb, kt.astype(BF16))
            if fox:
                cs = []
                for r in pages:
                    cs.append(pc_pages[r][...] + carry)
                    carry = cs[-1][:, page - 1:page]
                neg_c = _pad_rows(-jnp.concatenate(cs, axis=1), BF16_ROWS)
                bias = _dot3(expand, neg_c)
                s = s + bias
                if j == blk_per_step - 1:
                    ctot_ref[...] = bias[:, MOBA_BLOCK - 1:MOBA_BLOCK]
            else:
                kbar = jnp.mean(kt, axis=1, keepdims=True)
                sblk_ref[pl.ds(c * blk_per_step + j, 1), :] = jnp.sum(kbar * qf_ref[...], axis=0, keepdims=True)
            sc_ref[c * blk_per_step + j] = s
        if fox:
            carry_ref[...] = carry

    @pl.when(c == n_chunk - 1)
    def _():
        s_new = _dot_nt(qb_ref[...], _pad_rows(kn_ref[...], MOBA_BLOCK).astype(BF16))
        key = lax.broadcasted_iota(jnp.int32, (nrow, MOBA_BLOCK), 1)
        qry = lax.broadcasted_iota(jnp.int32, (nrow, MOBA_BLOCK), 0) % tq
        if fox:
            lane8 = lax.broadcasted_iota(jnp.int32, (nrow, nh), 1)
            head8 = lax.broadcasted_iota(jnp.int32, (nrow, nh), 0) // tq
            cs = jnp.zeros((1, nh), F32)
            bias = jnp.broadcast_to(ctot_ref[...], (nrow, MOBA_BLOCK))
            for i in range(tq):
                cs = cs + lfn_ref[i:i + 1, :]
                col = jnp.sum(jnp.where(lane8 == head8, jnp.broadcast_to(cs, (nrow, nh)), 0.0), axis=1, keepdims=True)
                bias = jnp.where(key == i, bias - col, bias)
            s_new = s_new + bias
        sc_ref[n_blk] = jnp.where((key <= qry) & (key < tq), s_new, NEG)

        if fox:
            blk_bias_t = None
        else:
            sb = sblk_ref[...]
            n_idx = lax.broadcasted_iota(jnp.int32, sb.shape, 0)
            rank = jnp.zeros(sb.shape, jnp.int32)
            for m_i in range(n_blk):
                sm = sb[m_i:m_i + 1, :]
                rank = rank + ((sm > sb) | ((sm == sb) & (m_i < n_idx))).astype(jnp.int32)
            blk_bias_t = _pad_rows(jnp.where(rank < MOBA_TOPK, 0.0, NEG), nrow).T

        def block(n):
            s = sc_ref[n]
            if blk_bias_t is not None and n < n_blk:
                s = s + blk_bias_t[:, n:n + 1]
            return s

        m = None
        for n in range(n_blk + 1):
            bm = jnp.max(block(n), axis=1, keepdims=True)
            m = bm if m is None else jnp.maximum(m, bm)
        l = jnp.zeros_like(m)
        for n in range(n_blk + 1):
            p = jnp.exp(block(n) - m)
            l = l + jnp.sum(p, axis=1, keepdims=True)
            p_ref[n] = p.astype(BF16)
        l_ref[...] = l

    @pl.when(c >= n_chunk)
    def _():
        acc = acc_ref[...]
        for j in range(blk_per_step):
            pages = range(j * pg_per_blk, (j + 1) * pg_per_blk)
            vt = jnp.concatenate([v_pages[r][...] for r in pages], axis=1).astype(BF16)
            acc = acc + _dot_nt(p_ref[(c - n_chunk) * blk_per_step + j], vt)
        acc_ref[...] = acc

    @pl.when(c == 2 * n_chunk - 1)
    def _():
        acc = acc_ref[...] + _dot(p_ref[n_blk], _pad_rows(vn_ref[...], MOBA_BLOCK).astype(BF16))
        o_full = acc / l_ref[...]
        lane_head = lax.broadcasted_iota(jnp.int32, (tq, w), 1) // HEAD_DIM
        out = jnp.zeros((tq, w), F32)
        for h in range(nh):
            out = out + jnp.where(lane_head == h, o_full[h * tq:(h + 1) * tq, :], 0.0)
        o_ref[...] = out.astype(o_ref.dtype)


def _sample_attn(page_table, q, k_new, v_new, cache_kt, cache_vt, layer, lf_new=None, page_cs=None):
    fox = lf_new is not None
    b, tq, w = q.shape
    nh = w // HEAD_DIM
    page = cache_kt.shape[3]
    npg = PAGES_PER_STEP
    n_pages = page_table.shape[0] // b
    n_chunk = n_pages // npg
    n_blk = n_pages * page // MOBA_BLOCK
    nrow = SCORE_COLS

    def k_page(r):
        return lambda i, c, pt: (layer, pt[i * n_pages + jnp.minimum(c, n_chunk - 1) * npg + r], 0, 0)

    def v_page(r):
        def index(i, c, pt):
            seq = jnp.where(c >= n_chunk, i, jnp.maximum(i - 1, 0))
            chunk = jnp.where(c >= n_chunk, c - n_chunk, jnp.where(i > 0, n_chunk - 1, 0))
            return (layer, pt[seq * n_pages + chunk * npg + r], 0, 0)
        return index

    new_spec = pl.BlockSpec((None, None, tq, w), lambda i, c, pt: (layer, i, 0, 0))
    in_specs = [pl.BlockSpec((None, tq, w), lambda i, c, pt: (i, 0, 0)), new_spec, new_spec]
    args = [q, k_new, v_new]
    if fox:
        in_specs.append(pl.BlockSpec((None, None, tq, nh), lambda i, c, pt: (layer, i, 0, 0)))
        args.append(lf_new)
    in_specs += [pl.BlockSpec((None, None, w, page), k_page(r)) for r in range(npg)]
    in_specs += [pl.BlockSpec((None, None, w, page), v_page(r)) for r in range(npg)]
    args += [cache_kt] * npg + [cache_vt] * npg
    if fox:
        in_specs += [pl.BlockSpec((None, None, nh, page), k_page(r)) for r in range(npg)]
        args += [page_cs] * npg
    grid_spec = pltpu.PrefetchScalarGridSpec(
        num_scalar_prefetch=1, grid=(b, 2 * n_chunk), in_specs=in_specs,
        out_specs=pl.BlockSpec((None, tq, w), lambda i, c, pt: (i, 0, 0)),
        scratch_shapes=[pltpu.VMEM((nrow, w), BF16), pltpu.VMEM((w, nrow), F32),
                        pltpu.VMEM((n_blk + 1, nrow, MOBA_BLOCK), F32), pltpu.VMEM((n_blk + 1, nrow, MOBA_BLOCK), BF16),
                        pltpu.VMEM((n_blk, nrow), F32), pltpu.VMEM((nrow, w), F32), pltpu.VMEM((nrow, 1), F32),
                        pltpu.VMEM((nh, 1), F32), pltpu.VMEM((nrow, 1), F32)])
    return pl.pallas_call(
        functools.partial(_sample_attn_kernel, fox=fox, n_chunk=n_chunk), grid_spec=grid_spec,
        out_shape=jax.ShapeDtypeStruct((b, tq, w), BF16),
        compiler_params=_params(("arbitrary", "arbitrary")), name="fox_sample" if fox else "moba_sample",
    )(page_table, *args)


def _group_layer(x, layer, depth, wts, stacked, mixers, tm, final_norm, seq_len=None):
    b_f = wts["b_f"].reshape(-1, 1) if seq_len is not None else wts["b_f"].reshape(1, -1)
    qa, u, qc, stacked = _project(x, wts["norm1"], wts["w_t"], wts["wf_t"], b_f, stacked, layer, depth, tm, seq_len)
    a, b, c = mixers(qa, u, qc, stacked)
    x = _merge(x, wts["norm1"], a, b, c, wts["wg_t"], wts["w_a"], wts["w_b"], wts["w_c"], wts["w_o"], tm)
    x = _ffn(x, wts["norm2"], wts["w_fi"], wts["w_fo"], wts["norm_f"], final_norm, tm)
    return x, u, stacked


def kernel(x_prompt, x_sample, cache_a_k, cache_a_v, cache_c_k, cache_c_v, cache_c_logf, state_pool, page_table,
           norm1, w_in, b_f, w_pool, pool_scale, w_br_a, w_br_b, w_br_c, w_out, norm2, w_ffn_in, w_ffn_out, norm_f):
    bp, tp, d = x_prompt.shape
    bs, ts, _ = x_sample.shape
    depth, n_phys, page, nh, hd = cache_a_k.shape
    w = nh * hd
    n_state = state_pool.shape[2]
    past_len = page_table.shape[1] * page
    n_main = 7 * w

    paged = lambda cache: cache.transpose(0, 1, 3, 4, 2).reshape(depth, n_phys, w, page)
    cak, cav, cck, ccv = paged(cache_a_k), paged(cache_a_v), paged(cache_c_k), paged(cache_c_v)
    logf_rows = cache_c_logf.transpose(0, 1, 3, 2).reshape(depth * n_phys * nh, page)
    page_cs = _page_cumsum(logf_rows, 4096).reshape(depth, n_phys, nh, page)
    pt_flat = page_table.reshape(-1)
    hist_p = jnp.zeros((bp, POOL_HIST, w), F32)
    hist_s = jnp.pad(state_pool, ((0, 0), (0, 0), (POOL_HIST - n_state, 0), (0, 0)))
    w_in_t = jnp.swapaxes(w_in, 1, 2).astype(BF16)

    xp, xs = x_prompt.reshape(bp * tp, d), x_sample.reshape(bs * ts, d)
    st_p = st_s = None
    pool_p, pool_s = [], []
    for l in range(depth):
        wts = dict(norm1=norm1[l][None], norm2=norm2[l][None], norm_f=norm_f[None], b_f=b_f[l],
                   w_t=w_in_t[l, :n_main], wf_t=_pad_rows(w_in_t[l, n_main:n_main + nh], BF16_ROWS),
                   wg_t=w_in_t[l, n_main + nh:], w_a=w_br_a[l].astype(BF16),
                   w_b=w_br_b[l].astype(BF16), w_c=w_br_c[l].astype(BF16), w_o=w_out[l].astype(BF16),
                   w_fi=w_ffn_in[l].astype(BF16), w_fo=w_ffn_out[l].astype(BF16))
        wp, scale = w_pool[l].astype(BF16), pool_scale[l][None]
        last = l == depth - 1

        def prompt_mixers(qa, u, qc, st):
            ka_t, va_t, kc_t, vc_t, lf_t = st
            qa3, qc3 = qa.reshape(bp, tp, w), qc.reshape(bp, tp, w)
            a = _prompt_attn(qa3, ka_t, va_t, _moba_prep(qa3, ka_t, l), l, fox=False)
            b = _pool(u.reshape(bp, tp, w), hist_p, wp, scale, 0, 1)
            c = _prompt_attn(qc3, kc_t, vc_t, _fox_prep(lf_t, l), l, fox=True)
            return a.reshape(bp * tp, w), b.reshape(bp * tp, w), c.reshape(bp * tp, w)

        def sample_mixers(qa, u, qc, st):
            ka, va, kc, vc, lf = (s.reshape(depth, bs, ts, -1) for s in st)
            a = _sample_attn(pt_flat, qa.reshape(bs, ts, w), ka, va, cak, cav, l)
            b = _pool(u.reshape(bs, ts, w), hist_s[l], wp, scale, past_len, bs)
            c = _sample_attn(pt_flat, qc.reshape(bs, ts, w), kc, vc, cck, ccv, l, lf_new=lf, page_cs=page_cs)
            return a.reshape(bs * ts, w), b.reshape(bs * ts, w), c.reshape(bs * ts, w)

        xp, u_p, st_p = _group_layer(xp, l, depth, wts, st_p, prompt_mixers, 512, last, seq_len=tp)
        xs, u_s, st_s = _group_layer(xs, l, depth, wts, st_s, sample_mixers, bs * ts, last)
        pool_p.append(u_p.reshape(bp, tp, w)[:, tp - n_state:])
        pool_s.append(jnp.concatenate([state_pool[l], u_s.reshape(bs, ts, w)], axis=1)[:, ts:])

    heads_p = lambda s: s.reshape(depth, bp, nh, hd, tp).transpose(0, 1, 4, 2, 3)
    heads_s = lambda s: s.reshape(depth, bs, ts, nh, hd)
    return (xp.reshape(bp, tp, d), xs.reshape(bs, ts, d),
            heads_p(st_p[0]), heads_p(st_p[1]), heads_p(st_p[2]), heads_p(st_p[3]),
            st_p[4].transpose(0, 1, 3, 2), jnp.stack(pool_p),
            heads_s(st_s[0]), heads_s(st_s[1]), heads_s(st_s[2]), heads_s(st_s[3]),
            st_s[4].reshape(depth, bs, ts, nh), jnp.stack(pool_s))
```

```python
import functools

import jax
import jax.numpy as jnp
from jax import lax
from jax.experimental import pallas as pl
from jax.experimental.pallas import tpu as pltpu

F32 = jnp.float32
BF16 = jnp.bfloat16

HEAD_DIM = 64
PAIR_W = 2 * HEAD_DIM
MOBA_BLOCK = 256
MOBA_TOPK = 3
POOL_WINDOWS = (2, 4, 8, 16)
POOL_GROUP_W = 128
POOL_HIST = max(POOL_WINDOWS)
RMS_EPS = 1e-6
ATTN_SCALE = HEAD_DIM ** -0.5
NEG = -1e30
QTILE = 256
PAGES_PER_STEP = 32
BF16_ROWS = 16
SCORE_COLS = 128
V7X_VMEM_LIMIT = 56 * 1024 * 1024


def _params(sem):
    return pltpu.CompilerParams(dimension_semantics=sem, vmem_limit_bytes=V7X_VMEM_LIMIT)


def _rmsnorm(x, g):
    ms = jnp.mean(x * x, axis=-1, keepdims=True)
    return x * lax.rsqrt(ms + RMS_EPS) * g


def _log_sigmoid(x):
    return -(jnp.maximum(-x, 0.0) + jnp.log1p(jnp.exp(-jnp.abs(x))))


def _dot(a, b):
    return jnp.dot(a, b, preferred_element_type=F32)


def _dot_nt(a, b, precision=None):
    return lax.dot_general(a, b, (((1,), (1,)), ((), ())), preferred_element_type=F32, precision=precision)


def _split3(x):
    hi = x.astype(BF16)
    r = x - hi.astype(F32)
    mid = r.astype(BF16)
    lo = (r - mid.astype(F32)).astype(BF16)
    return hi, mid, lo


def _dot3(a, x):
    hi, mid, lo = _split3(x)
    return _dot(a, hi) + _dot(a, mid) + _dot(a, lo)


def _pad_rows(x, rows):
    return jnp.concatenate([x, jnp.zeros((rows - x.shape[0],) + x.shape[1:], x.dtype)], axis=0)


def _upper_tri(n):
    r = lax.broadcasted_iota(jnp.int32, (n, n), 0)
    c = lax.broadcasted_iota(jnp.int32, (n, n), 1)
    return (r <= c).astype(BF16)


def _proj_kernel(*refs, n_alias, transposed):
    x_ref, g_ref, w_ref, wf_ref, bf_ref = refs[:5]
    qa_ref, ka_ref, va_ref, u_ref, qc_ref, kc_ref, vc_ref, lf_ref = refs[5 + n_alias:]
    w = qa_ref.shape[-1]
    nh = w // HEAD_DIM
    hb = _rmsnorm(x_ref[...], g_ref[...]).astype(BF16)
    for j, o_ref in ((0, qa_ref), (3, u_ref), (4, qc_ref)):
        o_ref[...] = _dot_nt(hb, w_ref[j * w:(j + 1) * w, :])
    if transposed:
        for j, k_ref, v_ref in ((1, ka_ref, va_ref), (5, kc_ref, vc_ref)):
            kv = _dot_nt(w_ref[j * w:(j + 2) * w, :], hb)
            k_ref[...] = kv[0:w]
            v_ref[...] = kv[w:2 * w]
        lf_ref[...] = _log_sigmoid(_dot_nt(wf_ref[...], hb)[0:nh] + bf_ref[...])
    else:
        for j, o_ref in ((1, ka_ref), (2, va_ref), (5, kc_ref), (6, vc_ref)):
            o_ref[...] = _dot_nt(hb, w_ref[j * w:(j + 1) * w, :])
        lf_ref[...] = _log_sigmoid(_dot_nt(hb, wf_ref[...])[:, 0:nh] + bf_ref[...])


def _project(x, g, w_t, wf_t, b_f, stacked, layer, depth, tm, seq_len=None):
    m, d = x.shape
    w = w_t.shape[0] // 7
    nh = w // HEAD_DIM
    transposed = seq_len is not None
    row = lambda i: (i, 0)
    const = lambda i: (0, 0)
    in_specs = [pl.BlockSpec((tm, d), row), pl.BlockSpec((1, d), const), pl.BlockSpec(w_t.shape, const),
                pl.BlockSpec(wf_t.shape, const), pl.BlockSpec(b_f.shape, const)]
    args = [x, g, w_t, wf_t, b_f]
    aliases = {}
    if stacked is not None:
        in_specs += [pl.BlockSpec(memory_space=pl.ANY)] * 5
        args += list(stacked)
        aliases = {5: 1, 6: 2, 7: 5, 8: 6, 9: 7}
    plain = pl.BlockSpec((tm, w), row)
    plain_s = jax.ShapeDtypeStruct((m, w), F32)
    if transposed:
        nt = seq_len // tm
        lay = lambda i: (layer, i // nt, 0, i % nt)
        stack, lf_spec = pl.BlockSpec((None, None, w, tm), lay), pl.BlockSpec((None, None, nh, tm), lay)
        stack_s = jax.ShapeDtypeStruct((depth, m // seq_len, w, seq_len), F32)
        lf_s = jax.ShapeDtypeStruct((depth, m // seq_len, nh, seq_len), F32)
    else:
        lay = lambda i: (layer, i, 0)
        stack, lf_spec = pl.BlockSpec((None, tm, w), lay), pl.BlockSpec((None, tm, nh), lay)
        stack_s = jax.ShapeDtypeStruct((depth, m, w), F32)
        lf_s = jax.ShapeDtypeStruct((depth, m, nh), F32)
    qa, ka, va, u, qc, kc, vc, lf = pl.pallas_call(
        functools.partial(_proj_kernel, n_alias=len(aliases), transposed=transposed), grid=(m // tm,),
        in_specs=in_specs, out_specs=[plain, stack, stack, plain, plain, stack, stack, lf_spec],
        out_shape=[plain_s, stack_s, stack_s, plain_s, plain_s, stack_s, stack_s, lf_s],
        input_output_aliases=aliases, compiler_params=_params(("arbitrary",)), name="proj",
    )(*args)
    return qa, u, qc, (ka, va, kc, vc, lf)


def _merge_kernel(x_ref, g_ref, a_ref, b_ref, c_ref, wg_ref, wa_ref, wb_ref, wc_ref, wo_ref, o_ref):
    x = x_ref[...]
    d = x.shape[-1]
    hb = _rmsnorm(x, g_ref[...]).astype(BF16)
    y = None
    for i, (br_ref, w_ref) in enumerate(((a_ref, wa_ref), (b_ref, wb_ref), (c_ref, wc_ref))):
        gate = jax.nn.sigmoid(_dot_nt(hb, wg_ref[i * d:(i + 1) * d, :]))
        t = gate * _dot(br_ref[...], w_ref[...])
        y = t if y is None else y + t
    o_ref[...] = x + _dot(y.astype(BF16), wo_ref[...])


def _merge(x, g, a, b, c, wg_t, w_a, w_b, w_c, w_o, tm):
    m, d = x.shape
    row = lambda i: (i, 0)
    const = lambda i: (0, 0)
    full = lambda arr: pl.BlockSpec(arr.shape, const)
    br = lambda arr: pl.BlockSpec((tm, arr.shape[1]), row)
    return pl.pallas_call(
        _merge_kernel, grid=(m // tm,),
        in_specs=[pl.BlockSpec((tm, d), row), pl.BlockSpec((1, d), const), br(a), br(b), br(c),
                  full(wg_t), full(w_a), full(w_b), full(w_c), full(w_o)],
        out_specs=pl.BlockSpec((tm, d), row), out_shape=jax.ShapeDtypeStruct((m, d), F32),
        compiler_params=_params(("arbitrary",)), name="merge",
    )(x, g, a, b, c, wg_t, w_a, w_b, w_c, w_o)


def _ffn_kernel(x_ref, g_ref, wi_ref, wo_ref, gf_ref, o_ref, *, chunk, final_norm):
    x = x_ref[...]
    d_ff = wo_ref.shape[0]
    hb = _rmsnorm(x, g_ref[...]).astype(BF16)
    acc = x
    for c0 in range(0, d_ff, chunk):
        gt = _dot(hb, wi_ref[:, c0:c0 + chunk])
        up = _dot(hb, wi_ref[:, d_ff + c0:d_ff + c0 + chunk])
        act = (gt * jax.nn.sigmoid(gt) * up).astype(BF16)
        acc = acc + _dot(act, wo_ref[c0:c0 + chunk, :])
    o_ref[...] = _rmsnorm(acc, gf_ref[...]) if final_norm else acc


def _ffn(x, g, w_i, w_o, g_final, final_norm, tm, chunk=256):
    m, d = x.shape
    row = lambda i: (i, 0)
    const = lambda i: (0, 0)
    return pl.pallas_call(
        functools.partial(_ffn_kernel, chunk=chunk, final_norm=final_norm), grid=(m // tm,),
        in_specs=[pl.BlockSpec((tm, d), row), pl.BlockSpec((1, d), const), pl.BlockSpec(w_i.shape, const),
                  pl.BlockSpec(w_o.shape, const), pl.BlockSpec((1, d), const)],
        out_specs=pl.BlockSpec((tm, d), row), out_shape=jax.ShapeDtypeStruct((m, d), F32),
        compiler_params=_params(("arbitrary",)), name="ffn",
    )(x, g, w_i, w_o, g_final)


def _pool_kernel(u_ref, hist_ref, w_ref, scale_ref, o_ref, s_ref, *, pos0):
    nb, t, _ = u_ref.shape
    ext = t + POOL_HIST
    pos = pos0 + lax.broadcasted_iota(jnp.int32, (1, t, 1), 1)
    for gi, win in enumerate(POOL_WINDOWS):
        sl = slice(gi * POOL_GROUP_W, (gi + 1) * POOL_GROUP_W)
        u = u_ref[:, :, sl]
        s_ref[:, 0:POOL_HIST, :] = hist_ref[:, :, sl]
        s_ref[:, POOL_HIST:, :] = u
        shift = 1
        while shift < win:
            s_ref[:, shift:, :] = s_ref[:, shift:, :] + s_ref[:, 0:ext - shift, :]
            shift *= 2
        cnt = jnp.minimum(win, pos + 1).astype(F32)
        dlt = s_ref[:, POOL_HIST:, :] / cnt - u
        y = _dot(dlt.reshape(nb * t, POOL_GROUP_W).astype(BF16), w_ref[gi])
        o_ref[:, :, sl] = (y.reshape(nb, t, POOL_GROUP_W) * scale_ref[:, sl]).astype(o_ref.dtype)


def _pool(u, hist, w_pool, scale, pos0, nb):
    b, t, w = u.shape
    blk = lambda i: (i, 0, 0)
    return pl.pallas_call(
        functools.partial(_pool_kernel, pos0=pos0), grid=(b // nb,),
        in_specs=[pl.BlockSpec((nb, t, w), blk), pl.BlockSpec((nb, POOL_HIST, w), blk),
                  pl.BlockSpec(w_pool.shape, lambda i: (0, 0, 0)), pl.BlockSpec((1, w), lambda i: (0, 0))],
        out_specs=pl.BlockSpec((nb, t, w), blk), out_shape=jax.ShapeDtypeStruct((b, t, w), BF16),
        scratch_shapes=[pltpu.VMEM((nb, t + POOL_HIST, POOL_GROUP_W), F32)],
        compiler_params=_params(("arbitrary",)), name="pool",
    )(u, hist, w_pool, scale)


def _fox_prep_kernel(lf_ref, kx_ref):
    nh, t = lf_ref.shape
    tri = _upper_tri(QTILE)
    carry = jnp.zeros((BF16_ROWS, 1), F32)
    parts = []
    for blk in range(t // QTILE):
        x = _pad_rows(lf_ref[:, blk * QTILE:(blk + 1) * QTILE], BF16_ROWS)
        hi, mid, lo = _split3(x)
        cs = _dot(hi, tri) + _dot(mid, tri) + _dot(lo, tri) + carry
        carry = cs[:, QTILE - 1:QTILE]
        parts.append(cs)
    neg_c = -jnp.concatenate(parts, axis=1)
    terms = [term.astype(F32) for term in _split3(neg_c)]
    row = lax.broadcasted_iota(jnp.int32, (BF16_ROWS, t), 0)
    for p in range(nh // 2):
        out = jnp.zeros((BF16_ROWS, t), F32)
        for head in range(2):
            for part in range(3):
                src = terms[part][2 * p + head:2 * p + head + 1, :]
                out = jnp.where(row == 3 * head + part, src, out)
        kx_ref[p] = out.astype(BF16)


def _fox_prep(logf_t, layer):
    _, b, nh, t = logf_t.shape
    return pl.pallas_call(
        _fox_prep_kernel, grid=(b,),
        in_specs=[pl.BlockSpec((None, None, nh, t), lambda i: (layer, i, 0, 0))],
        out_specs=pl.BlockSpec((None, nh // 2, BF16_ROWS, t), lambda i: (i, 0, 0, 0)),
        out_shape=jax.ShapeDtypeStruct((b, nh // 2, BF16_ROWS, t), BF16),
        compiler_params=_params(("arbitrary",)), name="fox_prep",
    )(logf_t)


def _moba_prep_kernel(q_ref, kt_ref, qx_ref):
    t, w = q_ref.shape
    nh, nb = w // HEAD_DIM, t // MOBA_BLOCK
    lane = lax.broadcasted_iota(jnp.int32, (w, PAIR_W), 1)
    kbar_t = jnp.zeros((w, PAIR_W), F32)
    for n in range(nb):
        col = jnp.mean(kt_ref[:, n * MOBA_BLOCK:(n + 1) * MOBA_BLOCK], axis=1, keepdims=True)
        kbar_t = jnp.where(lane == n, col, kbar_t)
    kbar = kbar_t.T[0:nb]
    head_of_lane = lax.broadcasted_iota(jnp.int32, (nh, nb, w), 2) // HEAD_DIM
    head = lax.broadcasted_iota(jnp.int32, (nh, nb, w), 0)
    kb = jnp.where(head_of_lane == head, kbar[None], 0.0).reshape(nh * nb, w)
    s = _dot_nt(kb, q_ref[...], precision=lax.Precision.HIGHEST).reshape(nh, nb, t)
    own = lax.broadcasted_iota(jnp.int32, (nh, nb, t), 2) // MOBA_BLOCK
    n_idx = lax.broadcasted_iota(jnp.int32, (nh, nb, t), 1)
    past = n_idx < own
    s = jnp.where(past, s, -jnp.inf)
    rank = jnp.zeros((nh, nb, t), jnp.int32)
    for m in range(nb):
        sm = s[:, m:m + 1, :]
        rank = rank + ((sm > s) | ((sm == s) & (m < n_idx))).astype(jnp.int32)
    keep = (past & (rank < MOBA_TOPK)) | (n_idx >= own)
    bias = jnp.where(keep, 0.0, NEG).reshape(nh * nb, t)
    for p in range(nh // 2):
        qx_ref[p] = _pad_rows(bias[2 * p * nb:(2 * p + 2) * nb], PAIR_W).T.astype(BF16)


def _moba_prep(q, k_t, layer):
    b, t, w = q.shape
    n_pair = w // PAIR_W
    return pl.pallas_call(
        _moba_prep_kernel, grid=(b,),
        in_specs=[pl.BlockSpec((None, t, w), lambda i: (i, 0, 0)),
                  pl.BlockSpec((None, None, w, t), lambda i: (layer, i, 0, 0))],
        out_specs=pl.BlockSpec((None, n_pair, t, PAIR_W), lambda i: (i, 0, 0, 0)),
        out_shape=jax.ShapeDtypeStruct((b, n_pair, t, PAIR_W), BF16),
        compiler_params=_params(("arbitrary",)), name="moba_prep",
    )(q, k_t)


def _prompt_attn_kernel(q_ref, kt_ref, vt_ref, x_ref, o_ref, qs_ref, kp_ref, vb_ref, *, fox):
    t = q_ref.shape[0]
    nt = t // QTILE
    lane = lax.broadcasted_iota(jnp.int32, (t, PAIR_W), 1)
    first = lane < HEAD_DIM
    qs = q_ref[...] * ATTN_SCALE
    heads_q = (jnp.where(first, qs, 0.0).astype(BF16), jnp.where(first, 0.0, qs).astype(BF16))
    if fox:
        tile_lane = lax.broadcasted_iota(jnp.int32, (QTILE, PAIR_W), 1)
        ones_x = (jnp.where(tile_lane < 3, 1.0, 0.0).astype(BF16),
                  jnp.where((tile_lane >= 3) & (tile_lane < 6), 1.0, 0.0).astype(BF16))
        heads_x = None
        kx_t = _pad_rows(x_ref[...].astype(F32), PAIR_W)
        kp_ref[...] = jnp.concatenate([kt_ref[...], kx_t], axis=0).T.astype(BF16)
    else:
        x = x_ref[...]
        zero = jnp.zeros_like(x)
        heads_x = (jnp.where(lane < nt, x, zero), jnp.where((lane >= nt) & (lane < 2 * nt), x, zero))
        row = lax.broadcasted_iota(jnp.int32, (t, PAIR_W), 0)
        kp_ref[:, 0:PAIR_W] = kt_ref[...].T.astype(BF16)
        kp_ref[:, PAIR_W:] = jnp.where((lane < 2 * nt) & (lane % nt == row // MOBA_BLOCK), 1.0, 0.0).astype(BF16)
    vb_ref[...] = vt_ref[...].astype(BF16)
    for i in range(nt):
        for h in range(2):
            qs_ref[i, h * QTILE:(h + 1) * QTILE, 0:PAIR_W] = heads_q[h][i * QTILE:(i + 1) * QTILE]
            qs_ref[i, h * QTILE:(h + 1) * QTILE, PAIR_W:] = (
                ones_x[h] if fox else heads_x[h][i * QTILE:(i + 1) * QTILE])

    key = lax.broadcasted_iota(jnp.int32, (QTILE, 2 * QTILE), 0)
    qry = lax.broadcasted_iota(jnp.int32, (QTILE, 2 * QTILE), 1) % QTILE
    causal = key <= qry
    first_out = lax.broadcasted_iota(jnp.int32, (PAIR_W, QTILE), 0) < HEAD_DIM
    def scores(i):
        lo = i * QTILE
        qi = qs_ref[i]
        s_diag = jnp.where(causal, _dot_nt(kp_ref[lo:lo + QTILE, :], qi), NEG)
        s_past = _dot_nt(kp_ref[0:lo, :], qi) if i > 0 else None
        return s_diag, s_past

    def probs(i, s_diag, s_past):
        m = jnp.max(s_diag, axis=0, keepdims=True)
        if i > 0:
            m = jnp.maximum(m, jnp.max(s_past, axis=0, keepdims=True))
        p_diag = jnp.exp(s_diag - m)
        l = jnp.sum(p_diag, axis=0, keepdims=True)
        p_past = None
        if i > 0:
            p_past = jnp.exp(s_past - m)
            l = l + jnp.sum(p_past, axis=0, keepdims=True)
            p_past = p_past.astype(BF16)
        return p_diag.astype(BF16), p_past, l

    def values(i, p_diag, p_past, l):
        lo = i * QTILE
        pv = _dot(vb_ref[:, lo:lo + QTILE], p_diag)
        if i > 0:
            pv = pv + _dot(vb_ref[:, 0:lo], p_past)
        o = pv / l
        o_ref[lo:lo + QTILE, :] = jnp.where(first_out, o[:, 0:QTILE], o[:, QTILE:]).T.astype(o_ref.dtype)

    pairs = [(a, nt - 1 - a) for a in range(nt // 2)]
    pending = None
    for pair in pairs:
        sc = [scores(i) for i in pair]
        if pending is not None:
            for i, p in pending:
                values(i, *p)
        pending = [(i, probs(i, *s)) for i, s in zip(pair, sc)]
    for i, p in pending:
        values(i, *p)


def _prompt_attn(q, k_t, v_t, x, layer, fox):
    b, t, w = q.shape
    n_pair = w // PAIR_W
    kv_spec = pl.BlockSpec((None, None, PAIR_W, t), lambda i, p: (layer, i, p, 0))
    x_spec = pl.BlockSpec((None, None) + x.shape[2:], lambda i, p: (i, p, 0, 0))
    return pl.pallas_call(
        functools.partial(_prompt_attn_kernel, fox=fox), grid=(b, n_pair),
        in_specs=[pl.BlockSpec((None, t, PAIR_W), lambda i, p: (i, 0, p)), kv_spec, kv_spec, x_spec],
        out_specs=pl.BlockSpec((None, t, PAIR_W), lambda i, p: (i, 0, p)),
        out_shape=jax.ShapeDtypeStruct((b, t, w), BF16),
        scratch_shapes=[pltpu.VMEM((t // QTILE, 2 * QTILE, 2 * PAIR_W), BF16),
                        pltpu.VMEM((t, 2 * PAIR_W), BF16), pltpu.VMEM((PAIR_W, t), BF16)],
        compiler_params=_params(("arbitrary", "arbitrary")), name="fox_attn" if fox else "moba_attn",
    )(q, k_t, v_t, x)


def _page_cumsum_kernel(x_ref, o_ref):
    o_ref[...] = _dot3_right(x_ref[...], _upper_tri(x_ref.shape[1]))


def _dot3_right(x, a):
    hi, mid, lo = _split3(x)
    return _dot(hi, a) + _dot(mid, a) + _dot(lo, a)


def _page_cumsum(logf_rows, tm):
    m, n = logf_rows.shape
    return pl.pallas_call(
        _page_cumsum_kernel, grid=(m // tm,),
        in_specs=[pl.BlockSpec((tm, n), lambda i: (i, 0))], out_specs=pl.BlockSpec((tm, n), lambda i: (i, 0)),
        out_shape=jax.ShapeDtypeStruct((m, n), F32),
        compiler_params=_params(("arbitrary",)), name="page_cumsum",
    )(logf_rows)


def _sample_attn_kernel(pt_ref, q_ref, kn_ref, vn_ref, *rest, fox, n_chunk):
    del pt_ref
    npg = PAGES_PER_STEP
    if fox:
        lfn_ref, negc_ref, rest = rest[0], rest[1], rest[2:]
    k_pages, v_pages = rest[:npg], rest[npg:2 * npg]
    o_ref, qb_ref, sc_ref, p_ref, sblk_ref, brep_ref, acc_ref, l_ref, ctot_ref = rest[2 * npg:]
    tq, w = q_ref.shape
    nh = w // HEAD_DIM
    nrow = qb_ref.shape[0]
    page = k_pages[0].shape[1]
    pg_per_blk = MOBA_BLOCK // page
    blk_per_step = npg // pg_per_blk
    n_blk = n_chunk * blk_per_step
    c = pl.program_id(1)

    @pl.when(c == 0)
    def _():
        rows = lax.broadcasted_iota(jnp.int32, (nrow, w), 0)
        lanes = lax.broadcasted_iota(jnp.int32, (nrow, w), 1)
        qrep = jnp.concatenate([q_ref[...]] * (nrow // tq), axis=0)
        qbd = jnp.where((rows // tq == lanes // HEAD_DIM) & (rows < nh * tq), qrep, 0.0)
        qb_ref[...] = (qbd * ATTN_SCALE).astype(BF16)
        acc_ref[...] = jnp.zeros_like(acc_ref)
        sblk_ref[...] = jnp.zeros_like(sblk_ref)

    @pl.when(c < n_chunk)
    def _():
        qb = qb_ref[...]
        if fox:
            erow = lax.broadcasted_iota(jnp.int32, (nrow, negc_ref.shape[0]), 0)
            ecol = lax.broadcasted_iota(jnp.int32, (nrow, negc_ref.shape[0]), 1)
            expand = ((ecol < 3 * nh) & (ecol % nh == erow // tq) & (erow < nh * tq)).astype(BF16)
        else:
            blk_lane = lax.broadcasted_iota(jnp.int32, (nrow, sblk_ref.shape[1]), 1)
            sblk = sblk_ref[...]
        for j in range(blk_per_step):
            pages = range(j * pg_per_blk, (j + 1) * pg_per_blk)
            kt = jnp.concatenate([k_pages[r][...] for r in pages], axis=1)
            s = _dot(qb, kt.astype(BF16))
            if fox:
                bias = _dot(expand, negc_ref[:, j * MOBA_BLOCK:(j + 1) * MOBA_BLOCK])
                s = s + bias
                if j == blk_per_step - 1:
                    ctot_ref[...] = bias[:, MOBA_BLOCK - 1:MOBA_BLOCK]
            else:
                half = s[:, 0:MOBA_BLOCK // 2] + s[:, MOBA_BLOCK // 2:]
                mean = jnp.sum(half, axis=1, keepdims=True) * (1.0 / MOBA_BLOCK)
                sblk = jnp.where(blk_lane == c * blk_per_step + j, mean, sblk)
            sc_ref[c * blk_per_step + j] = s
        if not fox:
            sblk_ref[...] = sblk

    @pl.when(c == n_chunk - 1)
    def _():
        s_new = _dot_nt(qb_ref[...], _pad_rows(kn_ref[...], MOBA_BLOCK).astype(BF16))
        key = lax.broadcasted_iota(jnp.int32, (nrow, MOBA_BLOCK), 1)
        qry = lax.broadcasted_iota(jnp.int32, (nrow, MOBA_BLOCK), 0) % tq
        if fox:
            lane8 = lax.broadcasted_iota(jnp.int32, (nrow, nh), 1)
            head8 = lax.broadcasted_iota(jnp.int32, (nrow, nh), 0) // tq
            cs = jnp.zeros((1, nh), F32)
            bias = jnp.broadcast_to(ctot_ref[...], (nrow, MOBA_BLOCK))
            for i in range(tq):
                cs = cs + lfn_ref[i:i + 1, :]
                col = jnp.sum(jnp.where(lane8 == head8, jnp.broadcast_to(cs, (nrow, nh)), 0.0), axis=1, keepdims=True)
                bias = jnp.where(key == i, bias - col, bias)
            s_new = s_new + bias
        sc_ref[n_blk] = jnp.where((key <= qry) & (key < tq), s_new, NEG)

        if not fox:
            sb = sblk_ref[...].T[0:n_blk]
            n_idx = lax.broadcasted_iota(jnp.int32, sb.shape, 0)
            rank = jnp.zeros(sb.shape, jnp.int32)
            for m_i in range(n_blk):
                sm = sb[m_i:m_i + 1, :]
                rank = rank + ((sm > sb) | ((sm == sb) & (m_i < n_idx))).astype(jnp.int32)
            bias_t = _pad_rows(jnp.where(rank < MOBA_TOPK, 0.0, NEG), sblk_ref.shape[1]).T
            rep_r = lax.broadcasted_iota(jnp.int32, (sblk_ref.shape[1], brep_ref.shape[1]), 0)
            rep_c = lax.broadcasted_iota(jnp.int32, (sblk_ref.shape[1], brep_ref.shape[1]), 1)
            brep_ref[...] = _dot(bias_t.astype(BF16), (rep_r == rep_c // SCORE_COLS).astype(BF16))

        def block(n):
            s = sc_ref[n]
            if not fox and n < n_blk:
                b = brep_ref[:, n * SCORE_COLS:(n + 1) * SCORE_COLS]
                s = s + jnp.concatenate([b] * (MOBA_BLOCK // SCORE_COLS), axis=1)
            return s

        mx = block(0)
        for n in range(1, n_blk + 1):
            mx = jnp.maximum(mx, block(n))
        m = jnp.broadcast_to(jnp.max(mx, axis=1, keepdims=True), mx.shape)
        tot = jnp.zeros_like(mx)
        for n in range(n_blk + 1):
            p = jnp.exp(block(n) - m)
            tot = tot + p
            p_ref[n] = p.astype(BF16)
        l_ref[...] = jnp.sum(tot, axis=1, keepdims=True)

    @pl.when(c >= n_chunk)
    def _():
        acc = acc_ref[...]
        for j in range(blk_per_step):
            pages = range(j * pg_per_blk, (j + 1) * pg_per_blk)
            vt = jnp.concatenate([v_pages[r][...] for r in pages], axis=1).astype(BF16)
            acc = acc + _dot_nt(p_ref[(c - n_chunk) * blk_per_step + j], vt)
        acc_ref[...] = acc

    @pl.when(c == 2 * n_chunk - 1)
    def _():
        acc = acc_ref[...] + _dot(p_ref[n_blk], _pad_rows(vn_ref[...], MOBA_BLOCK).astype(BF16))
        o_full = acc / l_ref[...]
        lane_head = lax.broadcasted_iota(jnp.int32, (tq, w), 1) // HEAD_DIM
        out = jnp.zeros((tq, w), F32)
        for h in range(nh):
            out = out + jnp.where(lane_head == h, o_full[h * tq:(h + 1) * tq, :], 0.0)
        o_ref[...] = out.astype(o_ref.dtype)


def _fox_cache_bias_kernel(pt_ref, pcs_ref, o_ref, *, n_pages):
    nh, page = pcs_ref.shape[1:]
    base = pl.program_id(0) * n_pages
    offs = jnp.zeros((nh, 1), F32)
    cols = []
    for j in range(n_pages):
        pg = pcs_ref[pt_ref[base + j]]
        cols.append(pg + offs)
        offs = offs + pg[:, page - 1:page]
    neg_c = -jnp.concatenate(cols, axis=1)
    terms = [t.astype(F32) for t in _split3(neg_c)] + [jnp.zeros_like(neg_c)]
    o_ref[...] = jnp.concatenate(terms, axis=0).astype(BF16)


def _fox_cache_bias(page_table, page_cs, layer, b):
    _, n_phys, nh, page = page_cs.shape
    n_pages = page_table.shape[0] // b
    grid_spec = pltpu.PrefetchScalarGridSpec(
        num_scalar_prefetch=1, grid=(b,),
        in_specs=[pl.BlockSpec((None, n_phys, nh, page), lambda i, pt: (layer, 0, 0, 0))],
        out_specs=pl.BlockSpec((None, 4 * nh, n_pages * page), lambda i, pt: (i, 0, 0)))
    return pl.pallas_call(
        functools.partial(_fox_cache_bias_kernel, n_pages=n_pages), grid_spec=grid_spec,
        out_shape=jax.ShapeDtypeStruct((b, 4 * nh, n_pages * page), BF16),
        compiler_params=_params(("arbitrary",)), name="fox_cache_bias",
    )(page_table, page_cs)


def _sample_attn(page_table, q, k_new, v_new, cache_kt, cache_vt, layer, lf_new=None, neg_c=None):
    fox = lf_new is not None
    b, tq, w = q.shape
    nh = w // HEAD_DIM
    page = cache_kt.shape[3]
    npg = PAGES_PER_STEP
    n_pages = page_table.shape[0] // b
    n_chunk = n_pages // npg
    n_blk = n_pages * page // MOBA_BLOCK
    nrow = SCORE_COLS

    def k_page(r):
        return lambda i, c, pt: (layer, pt[i * n_pages + jnp.minimum(c, n_chunk - 1) * npg + r], 0, 0)

    def v_page(r):
        def index(i, c, pt):
            seq = jnp.where(c >= n_chunk, i, jnp.maximum(i - 1, 0))
            chunk = jnp.where(c >= n_chunk, c - n_chunk, jnp.where(i > 0, n_chunk - 1, 0))
            return (layer, pt[seq * n_pages + chunk * npg + r], 0, 0)
        return index

    new_spec = pl.BlockSpec((None, None, tq, w), lambda i, c, pt: (layer, i, 0, 0))
    in_specs = [pl.BlockSpec((None, tq, w), lambda i, c, pt: (i, 0, 0)), new_spec, new_spec]
    args = [q, k_new, v_new]
    if fox:
        in_specs.append(pl.BlockSpec((None, None, tq, nh), lambda i, c, pt: (layer, i, 0, 0)))
        in_specs.append(pl.BlockSpec((None, neg_c.shape[1], npg * page),
                                     lambda i, c, pt: (i, 0, jnp.minimum(c, n_chunk - 1))))
        args += [lf_new, neg_c]
    in_specs += [pl.BlockSpec((None, None, w, page), k_page(r)) for r in range(npg)]
    in_specs += [pl.BlockSpec((None, None, w, page), v_page(r)) for r in range(npg)]
    args += [cache_kt] * npg + [cache_vt] * npg
    blk_lanes = -(-n_blk // SCORE_COLS) * SCORE_COLS
    grid_spec = pltpu.PrefetchScalarGridSpec(
        num_scalar_prefetch=1, grid=(b, 2 * n_chunk), in_specs=in_specs,
        out_specs=pl.BlockSpec((None, tq, w), lambda i, c, pt: (i, 0, 0)),
        scratch_shapes=[pltpu.VMEM((nrow, w), BF16),
                        pltpu.VMEM((n_blk + 1, nrow, MOBA_BLOCK), F32), pltpu.VMEM((n_blk + 1, nrow, MOBA_BLOCK), BF16),
                        pltpu.VMEM((nrow, blk_lanes), F32), pltpu.VMEM((nrow, n_blk * SCORE_COLS), F32),
                        pltpu.VMEM((nrow, w), F32), pltpu.VMEM((nrow, 1), F32), pltpu.VMEM((nrow, 1), F32)])
    return pl.pallas_call(
        functools.partial(_sample_attn_kernel, fox=fox, n_chunk=n_chunk), grid_spec=grid_spec,
        out_shape=jax.ShapeDtypeStruct((b, tq, w), BF16),
        compiler_params=_params(("arbitrary", "arbitrary")), name="fox_sample" if fox else "moba_sample",
    )(page_table, *args)


def _group_layer(x, layer, depth, wts, stacked, mixers, tm, final_norm, seq_len=None):
    b_f = wts["b_f"].reshape(-1, 1) if seq_len is not None else wts["b_f"].reshape(1, -1)
    qa, u, qc, stacked = _project(x, wts["norm1"], wts["w_t"], wts["wf_t"], b_f, stacked, layer, depth, tm, seq_len)
    a, b, c = mixers(qa, u, qc, stacked)
    x = _merge(x, wts["norm1"], a, b, c, wts["wg_t"], wts["w_a"], wts["w_b"], wts["w_c"], wts["w_o"], tm)
    x = _ffn(x, wts["norm2"], wts["w_fi"], wts["w_fo"], wts["norm_f"], final_norm, tm)
    return x, u, stacked


def kernel(x_prompt, x_sample, cache_a_k, cache_a_v, cache_c_k, cache_c_v, cache_c_logf, state_pool, page_table,
           norm1, w_in, b_f, w_pool, pool_scale, w_br_a, w_br_b, w_br_c, w_out, norm2, w_ffn_in, w_ffn_out, norm_f):
    bp, tp, d = x_prompt.shape
    bs, ts, _ = x_sample.shape
    depth, n_phys, page, nh, hd = cache_a_k.shape
    w = nh * hd
    n_state = state_pool.shape[2]
    past_len = page_table.shape[1] * page
    n_main = 7 * w

    paged = lambda cache: cache.transpose(0, 1, 3, 4, 2).reshape(depth, n_phys, w, page)
    cak, cav, cck, ccv = paged(cache_a_k), paged(cache_a_v), paged(cache_c_k), paged(cache_c_v)
    logf_rows = cache_c_logf.transpose(0, 1, 3, 2).reshape(depth * n_phys * nh, page)
    page_cs = _page_cumsum(logf_rows, 4096).reshape(depth, n_phys, nh, page)
    pt_flat = page_table.reshape(-1)
    hist_p = jnp.zeros((bp, POOL_HIST, w), F32)
    hist_s = jnp.pad(state_pool, ((0, 0), (0, 0), (POOL_HIST - n_state, 0), (0, 0)))
    w_in_t = jnp.swapaxes(w_in, 1, 2).astype(BF16)

    xp, xs = x_prompt.reshape(bp * tp, d), x_sample.reshape(bs * ts, d)
    st_p = st_s = None
    pool_p, pool_s = [], []
    for l in range(depth):
        wts = dict(norm1=norm1[l][None], norm2=norm2[l][None], norm_f=norm_f[None], b_f=b_f[l],
                   w_t=w_in_t[l, :n_main], wf_t=_pad_rows(w_in_t[l, n_main:n_main + nh], BF16_ROWS),
                   wg_t=w_in_t[l, n_main + nh:], w_a=w_br_a[l].astype(BF16),
                   w_b=w_br_b[l].astype(BF16), w_c=w_br_c[l].astype(BF16), w_o=w_out[l].astype(BF16),
                   w_fi=w_ffn_in[l].astype(BF16), w_fo=w_ffn_out[l].astype(BF16))
        wp, scale = w_pool[l].astype(BF16), pool_scale[l][None]
        last = l == depth - 1

        def prompt_mixers(qa, u, qc, st):
            ka_t, va_t, kc_t, vc_t, lf_t = st
            qa3, qc3 = qa.reshape(bp, tp, w), qc.reshape(bp, tp, w)
            a = _prompt_attn(qa3, ka_t, va_t, _moba_prep(qa3, ka_t, l), l, fox=False)
            b = _pool(u.reshape(bp, tp, w), hist_p, wp, scale, 0, 1)
            c = _prompt_attn(qc3, kc_t, vc_t, _fox_prep(lf_t, l), l, fox=True)
            return a.reshape(bp * tp, w), b.reshape(bp * tp, w), c.reshape(bp * tp, w)

        def sample_mixers(qa, u, qc, st):
            ka, va, kc, vc, lf = (s.reshape(depth, bs, ts, -1) for s in st)
            a = _sample_attn(pt_flat, qa.reshape(bs, ts, w), ka, va, cak, cav, l)
            b = _pool(u.reshape(bs, ts, w), hist_s[l], wp, scale, past_len, bs)
            neg_c = _fox_cache_bias(pt_flat, page_cs, l, bs)
            c = _sample_attn(pt_flat, qc.reshape(bs, ts, w), kc, vc, cck, ccv, l, lf_new=lf, neg_c=neg_c)
            return a.reshape(bs * ts, w), b.reshape(bs * ts, w), c.reshape(bs * ts, w)

        xp, u_p, st_p = _group_layer(xp, l, depth, wts, st_p, prompt_mixers, 512, last, seq_len=tp)
        xs, u_s, st_s = _group_layer(xs, l, depth, wts, st_s, sample_mixers, bs * ts, last)
        pool_p.append(u_p.reshape(bp, tp, w)[:, tp - n_state:])
        pool_s.append(jnp.concatenate([state_pool[l], u_s.reshape(bs, ts, w)], axis=1)[:, ts:])

    heads_p = lambda s: s.reshape(depth, bp, nh, hd, tp).transpose(0, 1, 4, 2, 3)
    heads_s = lambda s: s.reshape(depth, bs, ts, nh, hd)
    return (xp.reshape(bp, tp, d), xs.reshape(bs, ts, d),
            heads_p(st_p[0]), heads_p(st_p[1]), heads_p(st_p[2]), heads_p(st_p[3]),
            st_p[4].transpose(0, 1, 3, 2), jnp.stack(pool_p),
            heads_s(st_s[0]), heads_s(st_s[1]), heads_s(st_s[2]), heads_s(st_s[3]),
            st_s[4].reshape(depth, bs, ts, nh), jnp.stack(pool_s))
```

```python
import functools

import jax
import jax.numpy as jnp
from jax import lax
from jax.experimental import pallas as pl
from jax.experimental.pallas import tpu as pltpu

F32 = jnp.float32
BF16 = jnp.bfloat16

HEAD_DIM = 64
PAIR_W = 2 * HEAD_DIM
MOBA_BLOCK = 256
MOBA_TOPK = 3
POOL_WINDOWS = (2, 4, 8, 16)
POOL_GROUP_W = 128
POOL_HIST = max(POOL_WINDOWS)
RMS_EPS = 1e-6
ATTN_SCALE = HEAD_DIM ** -0.5
NEG = -1e30
QTILE = 256
PAGES_PER_STEP = 32
BF16_ROWS = 16
SCORE_COLS = 128
V7X_VMEM_LIMIT = 56 * 1024 * 1024


def _params(sem):
    return pltpu.CompilerParams(dimension_semantics=sem, vmem_limit_bytes=V7X_VMEM_LIMIT)


def _rmsnorm(x, g):
    ms = jnp.mean(x * x, axis=-1, keepdims=True)
    return x * lax.rsqrt(ms + RMS_EPS) * g


def _log_sigmoid(x):
    return -(jnp.maximum(-x, 0.0) + jnp.log1p(jnp.exp(-jnp.abs(x))))


def _dot(a, b):
    return jnp.dot(a, b, preferred_element_type=F32)


def _dot_nt(a, b, precision=None):
    return lax.dot_general(a, b, (((1,), (1,)), ((), ())), preferred_element_type=F32, precision=precision)


def _split3(x):
    hi = x.astype(BF16)
    r = x - hi.astype(F32)
    mid = r.astype(BF16)
    lo = (r - mid.astype(F32)).astype(BF16)
    return hi, mid, lo


def _dot3(a, x):
    hi, mid, lo = _split3(x)
    return _dot(a, hi) + _dot(a, mid) + _dot(a, lo)


def _pad_rows(x, rows):
    return jnp.concatenate([x, jnp.zeros((rows - x.shape[0],) + x.shape[1:], x.dtype)], axis=0)


def _upper_tri(n):
    r = lax.broadcasted_iota(jnp.int32, (n, n), 0)
    c = lax.broadcasted_iota(jnp.int32, (n, n), 1)
    return (r <= c).astype(BF16)


def _proj_kernel(*refs, n_alias, transposed):
    x_ref, g_ref, w_ref, wf_ref, bf_ref = refs[:5]
    qa_ref, ka_ref, va_ref, u_ref, qc_ref, kc_ref, vc_ref, lf_ref = refs[5 + n_alias:]
    w = qa_ref.shape[-1]
    nh = w // HEAD_DIM
    hb = _rmsnorm(x_ref[...], g_ref[...]).astype(BF16)
    for j, o_ref in ((0, qa_ref), (3, u_ref), (4, qc_ref)):
        o_ref[...] = _dot_nt(hb, w_ref[j * w:(j + 1) * w, :])
    if transposed:
        for j, k_ref, v_ref in ((1, ka_ref, va_ref), (5, kc_ref, vc_ref)):
            kv = _dot_nt(w_ref[j * w:(j + 2) * w, :], hb)
            k_ref[...] = kv[0:w]
            v_ref[...] = kv[w:2 * w]
        lf_ref[...] = _log_sigmoid(_dot_nt(wf_ref[...], hb)[0:nh] + bf_ref[...])
    else:
        for j, o_ref in ((1, ka_ref), (2, va_ref), (5, kc_ref), (6, vc_ref)):
            o_ref[...] = _dot_nt(hb, w_ref[j * w:(j + 1) * w, :])
        lf_ref[...] = _log_sigmoid(_dot_nt(hb, wf_ref[...])[:, 0:nh] + bf_ref[...])


def _project(x, g, w_t, wf_t, b_f, stacked, layer, depth, tm, seq_len=None):
    m, d = x.shape
    w = w_t.shape[0] // 7
    nh = w // HEAD_DIM
    transposed = seq_len is not None
    row = lambda i: (i, 0)
    const = lambda i: (0, 0)
    in_specs = [pl.BlockSpec((tm, d), row), pl.BlockSpec((1, d), const), pl.BlockSpec(w_t.shape, const),
                pl.BlockSpec(wf_t.shape, const), pl.BlockSpec(b_f.shape, const)]
    args = [x, g, w_t, wf_t, b_f]
    aliases = {}
    if stacked is not None:
        in_specs += [pl.BlockSpec(memory_space=pl.ANY)] * 5
        args += list(stacked)
        aliases = {5: 1, 6: 2, 7: 5, 8: 6, 9: 7}
    plain = pl.BlockSpec((tm, w), row)
    plain_s = jax.ShapeDtypeStruct((m, w), F32)
    if transposed:
        nt = seq_len // tm
        lay = lambda i: (layer, i // nt, 0, i % nt)
        stack, lf_spec = pl.BlockSpec((None, None, w, tm), lay), pl.BlockSpec((None, None, nh, tm), lay)
        stack_s = jax.ShapeDtypeStruct((depth, m // seq_len, w, seq_len), F32)
        lf_s = jax.ShapeDtypeStruct((depth, m // seq_len, nh, seq_len), F32)
    else:
        lay = lambda i: (layer, i, 0)
        stack, lf_spec = pl.BlockSpec((None, tm, w), lay), pl.BlockSpec((None, tm, nh), lay)
        stack_s = jax.ShapeDtypeStruct((depth, m, w), F32)
        lf_s = jax.ShapeDtypeStruct((depth, m, nh), F32)
    qa, ka, va, u, qc, kc, vc, lf = pl.pallas_call(
        functools.partial(_proj_kernel, n_alias=len(aliases), transposed=transposed), grid=(m // tm,),
        in_specs=in_specs, out_specs=[plain, stack, stack, plain, plain, stack, stack, lf_spec],
        out_shape=[plain_s, stack_s, stack_s, plain_s, plain_s, stack_s, stack_s, lf_s],
        input_output_aliases=aliases, compiler_params=_params(("arbitrary",)), name="proj",
    )(*args)
    return qa, u, qc, (ka, va, kc, vc, lf)


def _merge_kernel(x_ref, g_ref, a_ref, b_ref, c_ref, wg_ref, wa_ref, wb_ref, wc_ref, wo_ref, o_ref):
    x = x_ref[...]
    d = x.shape[-1]
    hb = _rmsnorm(x, g_ref[...]).astype(BF16)
    y = None
    for i, (br_ref, w_ref) in enumerate(((a_ref, wa_ref), (b_ref, wb_ref), (c_ref, wc_ref))):
        gate = jax.nn.sigmoid(_dot_nt(hb, wg_ref[i * d:(i + 1) * d, :]))
        t = gate * _dot(br_ref[...], w_ref[...])
        y = t if y is None else y + t
    o_ref[...] = x + _dot(y.astype(BF16), wo_ref[...])


def _merge(x, g, a, b, c, wg_t, w_a, w_b, w_c, w_o, tm):
    m, d = x.shape
    row = lambda i: (i, 0)
    const = lambda i: (0, 0)
    full = lambda arr: pl.BlockSpec(arr.shape, const)
    br = lambda arr: pl.BlockSpec((tm, arr.shape[1]), row)
    return pl.pallas_call(
        _merge_kernel, grid=(m // tm,),
        in_specs=[pl.BlockSpec((tm, d), row), pl.BlockSpec((1, d), const), br(a), br(b), br(c),
                  full(wg_t), full(w_a), full(w_b), full(w_c), full(w_o)],
        out_specs=pl.BlockSpec((tm, d), row), out_shape=jax.ShapeDtypeStruct((m, d), F32),
        compiler_params=_params(("arbitrary",)), name="merge",
    )(x, g, a, b, c, wg_t, w_a, w_b, w_c, w_o)


def _ffn_kernel(x_ref, g_ref, wi_ref, wo_ref, gf_ref, o_ref, *, chunk, final_norm):
    x = x_ref[...]
    d_ff = wo_ref.shape[0]
    hb = _rmsnorm(x, g_ref[...]).astype(BF16)
    acc = x
    for c0 in range(0, d_ff, chunk):
        gt = _dot(hb, wi_ref[:, c0:c0 + chunk])
        up = _dot(hb, wi_ref[:, d_ff + c0:d_ff + c0 + chunk])
        act = (gt * jax.nn.sigmoid(gt) * up).astype(BF16)
        acc = acc + _dot(act, wo_ref[c0:c0 + chunk, :])
    o_ref[...] = _rmsnorm(acc, gf_ref[...]) if final_norm else acc


def _ffn(x, g, w_i, w_o, g_final, final_norm, tm, chunk=256):
    m, d = x.shape
    row = lambda i: (i, 0)
    const = lambda i: (0, 0)
    return pl.pallas_call(
        functools.partial(_ffn_kernel, chunk=chunk, final_norm=final_norm), grid=(m // tm,),
        in_specs=[pl.BlockSpec((tm, d), row), pl.BlockSpec((1, d), const), pl.BlockSpec(w_i.shape, const),
                  pl.BlockSpec(w_o.shape, const), pl.BlockSpec((1, d), const)],
        out_specs=pl.BlockSpec((tm, d), row), out_shape=jax.ShapeDtypeStruct((m, d), F32),
        compiler_params=_params(("arbitrary",)), name="ffn",
    )(x, g, w_i, w_o, g_final)


def _pool_kernel(u_ref, hist_ref, w_ref, scale_ref, o_ref, s_ref, *, pos0):
    nb, t, _ = u_ref.shape
    ext = t + POOL_HIST
    pos = pos0 + lax.broadcasted_iota(jnp.int32, (1, t, 1), 1)
    for gi, win in enumerate(POOL_WINDOWS):
        sl = slice(gi * POOL_GROUP_W, (gi + 1) * POOL_GROUP_W)
        u = u_ref[:, :, sl]
        s_ref[:, 0:POOL_HIST, :] = hist_ref[:, :, sl]
        s_ref[:, POOL_HIST:, :] = u
        shift = 1
        while shift < win:
            s_ref[:, shift:, :] = s_ref[:, shift:, :] + s_ref[:, 0:ext - shift, :]
            shift *= 2
        cnt = jnp.minimum(win, pos + 1).astype(F32)
        dlt = s_ref[:, POOL_HIST:, :] / cnt - u
        y = _dot(dlt.reshape(nb * t, POOL_GROUP_W).astype(BF16), w_ref[gi])
        o_ref[:, :, sl] = (y.reshape(nb, t, POOL_GROUP_W) * scale_ref[:, sl]).astype(o_ref.dtype)


def _pool(u, hist, w_pool, scale, pos0, nb):
    b, t, w = u.shape
    blk = lambda i: (i, 0, 0)
    return pl.pallas_call(
        functools.partial(_pool_kernel, pos0=pos0), grid=(b // nb,),
        in_specs=[pl.BlockSpec((nb, t, w), blk), pl.BlockSpec((nb, POOL_HIST, w), blk),
                  pl.BlockSpec(w_pool.shape, lambda i: (0, 0, 0)), pl.BlockSpec((1, w), lambda i: (0, 0))],
        out_specs=pl.BlockSpec((nb, t, w), blk), out_shape=jax.ShapeDtypeStruct((b, t, w), BF16),
        scratch_shapes=[pltpu.VMEM((nb, t + POOL_HIST, POOL_GROUP_W), F32)],
        compiler_params=_params(("arbitrary",)), name="pool",
    )(u, hist, w_pool, scale)


def _fox_prep_kernel(lf_ref, kx_ref):
    nh, t = lf_ref.shape
    tri = _upper_tri(QTILE)
    carry = jnp.zeros((BF16_ROWS, 1), F32)
    parts = []
    for blk in range(t // QTILE):
        x = _pad_rows(lf_ref[:, blk * QTILE:(blk + 1) * QTILE], BF16_ROWS)
        hi, mid, lo = _split3(x)
        cs = _dot(hi, tri) + _dot(mid, tri) + _dot(lo, tri) + carry
        carry = cs[:, QTILE - 1:QTILE]
        parts.append(cs)
    neg_c = -jnp.concatenate(parts, axis=1)
    terms = [term.astype(F32) for term in _split3(neg_c)]
    row = lax.broadcasted_iota(jnp.int32, (BF16_ROWS, t), 0)
    for p in range(nh // 2):
        out = jnp.zeros((BF16_ROWS, t), F32)
        for head in range(2):
            for part in range(3):
                src = terms[part][2 * p + head:2 * p + head + 1, :]
                out = jnp.where(row == 3 * head + part, src, out)
        kx_ref[p] = out.astype(BF16)


def _fox_prep(logf_t, layer):
    _, b, nh, t = logf_t.shape
    return pl.pallas_call(
        _fox_prep_kernel, grid=(b,),
        in_specs=[pl.BlockSpec((None, None, nh, t), lambda i: (layer, i, 0, 0))],
        out_specs=pl.BlockSpec((None, nh // 2, BF16_ROWS, t), lambda i: (i, 0, 0, 0)),
        out_shape=jax.ShapeDtypeStruct((b, nh // 2, BF16_ROWS, t), BF16),
        compiler_params=_params(("arbitrary",)), name="fox_prep",
    )(logf_t)


def _moba_prep_kernel(q_ref, kt_ref, qx_ref):
    t, w = q_ref.shape
    nh, nb = w // HEAD_DIM, t // MOBA_BLOCK
    lane = lax.broadcasted_iota(jnp.int32, (w, PAIR_W), 1)
    kbar_t = jnp.zeros((w, PAIR_W), F32)
    for n in range(nb):
        col = jnp.mean(kt_ref[:, n * MOBA_BLOCK:(n + 1) * MOBA_BLOCK], axis=1, keepdims=True)
        kbar_t = jnp.where(lane == n, col, kbar_t)
    kbar = kbar_t.T[0:nb]
    head_of_lane = lax.broadcasted_iota(jnp.int32, (nh, nb, w), 2) // HEAD_DIM
    head = lax.broadcasted_iota(jnp.int32, (nh, nb, w), 0)
    kb = jnp.where(head_of_lane == head, kbar[None], 0.0).reshape(nh * nb, w)
    s = _dot_nt(kb, q_ref[...], precision=lax.Precision.HIGHEST).reshape(nh, nb, t)
    own = lax.broadcasted_iota(jnp.int32, (nh, nb, t), 2) // MOBA_BLOCK
    n_idx = lax.broadcasted_iota(jnp.int32, (nh, nb, t), 1)
    past = n_idx < own
    s = jnp.where(past, s, -jnp.inf)
    rank = jnp.zeros((nh, nb, t), jnp.int32)
    for m in range(nb):
        sm = s[:, m:m + 1, :]
        rank = rank + ((sm > s) | ((sm == s) & (m < n_idx))).astype(jnp.int32)
    keep = (past & (rank < MOBA_TOPK)) | (n_idx >= own)
    bias = jnp.where(keep, 0.0, NEG).reshape(nh * nb, t)
    for p in range(nh // 2):
        qx_ref[p] = _pad_rows(bias[2 * p * nb:(2 * p + 2) * nb], PAIR_W).T.astype(BF16)


def _moba_prep(q, k_t, layer):
    b, t, w = q.shape
    n_pair = w // PAIR_W
    return pl.pallas_call(
        _moba_prep_kernel, grid=(b,),
        in_specs=[pl.BlockSpec((None, t, w), lambda i: (i, 0, 0)),
                  pl.BlockSpec((None, None, w, t), lambda i: (layer, i, 0, 0))],
        out_specs=pl.BlockSpec((None, n_pair, t, PAIR_W), lambda i: (i, 0, 0, 0)),
        out_shape=jax.ShapeDtypeStruct((b, n_pair, t, PAIR_W), BF16),
        compiler_params=_params(("arbitrary",)), name="moba_prep",
    )(q, k_t)


def _prompt_attn_kernel(q_ref, kt_ref, vt_ref, x_ref, o_ref, qs_ref, kp_ref, vb_ref, *, fox):
    t = q_ref.shape[0]
    nt = t // QTILE
    lane = lax.broadcasted_iota(jnp.int32, (t, PAIR_W), 1)
    first = lane < HEAD_DIM
    qs = q_ref[...] * ATTN_SCALE
    heads_q = (jnp.where(first, qs, 0.0).astype(BF16), jnp.where(first, 0.0, qs).astype(BF16))
    if fox:
        tile_lane = lax.broadcasted_iota(jnp.int32, (QTILE, PAIR_W), 1)
        ones_x = (jnp.where(tile_lane < 3, 1.0, 0.0).astype(BF16),
                  jnp.where((tile_lane >= 3) & (tile_lane < 6), 1.0, 0.0).astype(BF16))
        heads_x = None
        kx_t = _pad_rows(x_ref[...].astype(F32), PAIR_W)
        kp_ref[...] = jnp.concatenate([kt_ref[...], kx_t], axis=0).T.astype(BF16)
    else:
        x = x_ref[...]
        zero = jnp.zeros_like(x)
        heads_x = (jnp.where(lane < nt, x, zero), jnp.where((lane >= nt) & (lane < 2 * nt), x, zero))
        row = lax.broadcasted_iota(jnp.int32, (t, PAIR_W), 0)
        kp_ref[:, 0:PAIR_W] = kt_ref[...].T.astype(BF16)
        kp_ref[:, PAIR_W:] = jnp.where((lane < 2 * nt) & (lane % nt == row // MOBA_BLOCK), 1.0, 0.0).astype(BF16)
    vb_ref[...] = vt_ref[...].astype(BF16)
    for i in range(nt):
        for h in range(2):
            qs_ref[i, h * QTILE:(h + 1) * QTILE, 0:PAIR_W] = heads_q[h][i * QTILE:(i + 1) * QTILE]
            qs_ref[i, h * QTILE:(h + 1) * QTILE, PAIR_W:] = (
                ones_x[h] if fox else heads_x[h][i * QTILE:(i + 1) * QTILE])

    key = lax.broadcasted_iota(jnp.int32, (QTILE, 2 * QTILE), 0)
    qry = lax.broadcasted_iota(jnp.int32, (QTILE, 2 * QTILE), 1) % QTILE
    causal = key <= qry
    first_out = lax.broadcasted_iota(jnp.int32, (PAIR_W, QTILE), 0) < HEAD_DIM
    def scores(i):
        lo = i * QTILE
        qi = qs_ref[i]
        s_diag = jnp.where(causal, _dot_nt(kp_ref[lo:lo + QTILE, :], qi), NEG)
        s_past = _dot_nt(kp_ref[0:lo, :], qi) if i > 0 else None
        return s_diag, s_past

    def probs(i, s_diag, s_past):
        m = jnp.max(s_diag, axis=0, keepdims=True)
        if i > 0:
            m = jnp.maximum(m, jnp.max(s_past, axis=0, keepdims=True))
        p_diag = jnp.exp(s_diag - m)
        l = jnp.sum(p_diag, axis=0, keepdims=True)
        p_past = None
        if i > 0:
            p_past = jnp.exp(s_past - m)
            l = l + jnp.sum(p_past, axis=0, keepdims=True)
            p_past = p_past.astype(BF16)
        return p_diag.astype(BF16), p_past, l

    def values(i, p_diag, p_past, l):
        lo = i * QTILE
        pv = _dot(vb_ref[:, lo:lo + QTILE], p_diag)
        if i > 0:
            pv = pv + _dot(vb_ref[:, 0:lo], p_past)
        o = pv / l
        o_ref[lo:lo + QTILE, :] = jnp.where(first_out, o[:, 0:QTILE], o[:, QTILE:]).T.astype(o_ref.dtype)

    pairs = [(a, nt - 1 - a) for a in range(nt // 2)]
    pending = None
    for pair in pairs:
        sc = [scores(i) for i in pair]
        if pending is not None:
            for i, p in pending:
                values(i, *p)
        pending = [(i, probs(i, *s)) for i, s in zip(pair, sc)]
    for i, p in pending:
        values(i, *p)


def _prompt_attn(q, k_t, v_t, x, layer, fox):
    b, t, w = q.shape
    n_pair = w // PAIR_W
    kv_spec = pl.BlockSpec((None, None, PAIR_W, t), lambda i, p: (layer, i, p, 0))
    x_spec = pl.BlockSpec((None, None) + x.shape[2:], lambda i, p: (i, p, 0, 0))
    return pl.pallas_call(
        functools.partial(_prompt_attn_kernel, fox=fox), grid=(b, n_pair),
        in_specs=[pl.BlockSpec((None, t, PAIR_W), lambda i, p: (i, 0, p)), kv_spec, kv_spec, x_spec],
        out_specs=pl.BlockSpec((None, t, PAIR_W), lambda i, p: (i, 0, p)),
        out_shape=jax.ShapeDtypeStruct((b, t, w), BF16),
        scratch_shapes=[pltpu.VMEM((t // QTILE, 2 * QTILE, 2 * PAIR_W), BF16),
                        pltpu.VMEM((t, 2 * PAIR_W), BF16), pltpu.VMEM((PAIR_W, t), BF16)],
        compiler_params=_params(("arbitrary", "arbitrary")), name="fox_attn" if fox else "moba_attn",
    )(q, k_t, v_t, x)


def _page_cumsum_kernel(x_ref, o_ref):
    o_ref[...] = _dot3_right(x_ref[...], _upper_tri(x_ref.shape[1]))


def _dot3_right(x, a):
    hi, mid, lo = _split3(x)
    return _dot(hi, a) + _dot(mid, a) + _dot(lo, a)


def _page_cumsum(logf_rows, tm):
    m, n = logf_rows.shape
    return pl.pallas_call(
        _page_cumsum_kernel, grid=(m // tm,),
        in_specs=[pl.BlockSpec((tm, n), lambda i: (i, 0))], out_specs=pl.BlockSpec((tm, n), lambda i: (i, 0)),
        out_shape=jax.ShapeDtypeStruct((m, n), F32),
        compiler_params=_params(("arbitrary",)), name="page_cumsum",
    )(logf_rows)


def _sample_attn_kernel(pt_ref, q_ref, kn_ref, vn_ref, *rest, fox, n_chunk, layer, n_pages):
    if fox:
        lfn_ref, negc_ref, rest = rest[0], rest[1], rest[2:]
    (ck_hbm, cv_hbm, o_ref, qb_ref, sc_ref, p_ref, sblk_ref, brep_ref, acc_ref, l_ref, ctot_ref,
     buf_ref, sem_ref) = rest
    tq, w = q_ref.shape
    nh = w // HEAD_DIM
    nrow = qb_ref.shape[0]
    npg, page = buf_ref.shape[1], buf_ref.shape[3]
    pg_per_blk = MOBA_BLOCK // page
    blk_per_step = npg // pg_per_blk
    n_blk = n_chunk * blk_per_step
    n_step = 2 * n_chunk
    i, c = pl.program_id(0), pl.program_id(1)
    slot = c % 2

    def page_copy(src_hbm, phys_page, sl, r):
        return pltpu.make_async_copy(src_hbm.at[layer, phys_page], buf_ref.at[sl, r], sem_ref.at[sl])

    def start_step(seq, step, sl):
        first = seq * n_pages + (step % n_chunk) * npg

        @pl.when(step < n_chunk)
        def _():
            for r in range(npg):
                page_copy(ck_hbm, pt_ref[first + r], sl, r).start()

        @pl.when(step >= n_chunk)
        def _():
            for r in range(npg):
                page_copy(cv_hbm, pt_ref[first + r], sl, r).start()

    @pl.when((i == 0) & (c == 0))
    def _():
        start_step(i, c, slot)

    wrap = c == n_step - 1

    @pl.when(jnp.logical_not(wrap & (i == pl.num_programs(0) - 1)))
    def _():
        start_step(jnp.where(wrap, i + 1, i), jnp.where(wrap, 0, c + 1), 1 - slot)

    for r in range(npg):
        page_copy(ck_hbm, 0, slot, r).wait()
    k_pages = v_pages = [buf_ref.at[slot, r] for r in range(npg)]

    @pl.when(c == 0)
    def _():
        rows = lax.broadcasted_iota(jnp.int32, (nrow, w), 0)
        lanes = lax.broadcasted_iota(jnp.int32, (nrow, w), 1)
        qrep = jnp.concatenate([q_ref[...]] * (nrow // tq), axis=0)
        qbd = jnp.where((rows // tq == lanes // HEAD_DIM) & (rows < nh * tq), qrep, 0.0)
        qb_ref[...] = (qbd * ATTN_SCALE).astype(BF16)
        acc_ref[...] = jnp.zeros_like(acc_ref)
        sblk_ref[...] = jnp.zeros_like(sblk_ref)

    @pl.when(c < n_chunk)
    def _():
        qb = qb_ref[...]
        if fox:
            erow = lax.broadcasted_iota(jnp.int32, (nrow, negc_ref.shape[0]), 0)
            ecol = lax.broadcasted_iota(jnp.int32, (nrow, negc_ref.shape[0]), 1)
            expand = ((ecol < 3 * nh) & (ecol % nh == erow // tq) & (erow < nh * tq)).astype(BF16)
        else:
            blk_lane = lax.broadcasted_iota(jnp.int32, (nrow, sblk_ref.shape[1]), 1)
            sblk = sblk_ref[...]
        for j in range(blk_per_step):
            pages = range(j * pg_per_blk, (j + 1) * pg_per_blk)
            kt = jnp.concatenate([k_pages[r][...] for r in pages], axis=1)
            s = _dot(qb, kt.astype(BF16))
            if fox:
                bias = _dot(expand, negc_ref[:, j * MOBA_BLOCK:(j + 1) * MOBA_BLOCK])
                s = s + bias
                if j == blk_per_step - 1:
                    ctot_ref[...] = bias[:, MOBA_BLOCK - 1:MOBA_BLOCK]
            else:
                half = s[:, 0:MOBA_BLOCK // 2] + s[:, MOBA_BLOCK // 2:]
                mean = jnp.sum(half, axis=1, keepdims=True) * (1.0 / MOBA_BLOCK)
                sblk = jnp.where(blk_lane == c * blk_per_step + j, mean, sblk)
            sc_ref[c * blk_per_step + j] = s
        if not fox:
            sblk_ref[...] = sblk

    @pl.when(c == n_chunk - 1)
    def _():
        s_new = _dot_nt(qb_ref[...], _pad_rows(kn_ref[...], MOBA_BLOCK).astype(BF16))
        key = lax.broadcasted_iota(jnp.int32, (nrow, MOBA_BLOCK), 1)
        qry = lax.broadcasted_iota(jnp.int32, (nrow, MOBA_BLOCK), 0) % tq
        if fox:
            lane8 = lax.broadcasted_iota(jnp.int32, (nrow, nh), 1)
            head8 = lax.broadcasted_iota(jnp.int32, (nrow, nh), 0) // tq
            cs = jnp.zeros((1, nh), F32)
            bias = jnp.broadcast_to(ctot_ref[...], (nrow, MOBA_BLOCK))
            for i in range(tq):
                cs = cs + lfn_ref[i:i + 1, :]
                col = jnp.sum(jnp.where(lane8 == head8, jnp.broadcast_to(cs, (nrow, nh)), 0.0), axis=1, keepdims=True)
                bias = jnp.where(key == i, bias - col, bias)
            s_new = s_new + bias
        sc_ref[n_blk] = jnp.where((key <= qry) & (key < tq), s_new, NEG)

        if not fox:
            sb = sblk_ref[...].T[0:n_blk]
            n_idx = lax.broadcasted_iota(jnp.int32, sb.shape, 0)
            rank = jnp.zeros(sb.shape, jnp.int32)
            for m_i in range(n_blk):
                sm = sb[m_i:m_i + 1, :]
                rank = rank + ((sm > sb) | ((sm == sb) & (m_i < n_idx))).astype(jnp.int32)
            bias_t = _pad_rows(jnp.where(rank < MOBA_TOPK, 0.0, NEG), sblk_ref.shape[1]).T
            rep_r = lax.broadcasted_iota(jnp.int32, (sblk_ref.shape[1], brep_ref.shape[1]), 0)
            rep_c = lax.broadcasted_iota(jnp.int32, (sblk_ref.shape[1], brep_ref.shape[1]), 1)
            brep_ref[...] = _dot(bias_t.astype(BF16), (rep_r == rep_c // SCORE_COLS).astype(BF16))

        def block(n):
            s = sc_ref[n]
            if not fox and n < n_blk:
                b = brep_ref[:, n * SCORE_COLS:(n + 1) * SCORE_COLS]
                s = s + jnp.concatenate([b] * (MOBA_BLOCK // SCORE_COLS), axis=1)
            return s

        mx = block(0)
        for n in range(1, n_blk + 1):
            mx = jnp.maximum(mx, block(n))
        m = jnp.broadcast_to(jnp.max(mx, axis=1, keepdims=True), mx.shape)
        tot = jnp.zeros_like(mx)
        for n in range(n_blk + 1):
            p = jnp.exp(block(n) - m)
            tot = tot + p
            p_ref[n] = p.astype(BF16)
        l_ref[...] = jnp.sum(tot, axis=1, keepdims=True)

    @pl.when(c >= n_chunk)
    def _():
        acc = acc_ref[...]
        for j in range(blk_per_step):
            pages = range(j * pg_per_blk, (j + 1) * pg_per_blk)
            vt = jnp.concatenate([v_pages[r][...] for r in pages], axis=1).astype(BF16)
            acc = acc + _dot_nt(p_ref[(c - n_chunk) * blk_per_step + j], vt)
        acc_ref[...] = acc

    @pl.when(c == 2 * n_chunk - 1)
    def _():
        acc = acc_ref[...] + _dot(p_ref[n_blk], _pad_rows(vn_ref[...], MOBA_BLOCK).astype(BF16))
        o_full = acc / l_ref[...]
        lane_head = lax.broadcasted_iota(jnp.int32, (tq, w), 1) // HEAD_DIM
        out = jnp.zeros((tq, w), F32)
        for h in range(nh):
            out = out + jnp.where(lane_head == h, o_full[h * tq:(h + 1) * tq, :], 0.0)
        o_ref[...] = out.astype(o_ref.dtype)


def _fox_cache_bias_kernel(pt_ref, pcs_ref, o_ref, *, n_pages):
    nh, page = pcs_ref.shape[1:]
    base = pl.program_id(0) * n_pages
    offs = jnp.zeros((nh, 1), F32)
    cols = []
    for j in range(n_pages):
        pg = pcs_ref[pt_ref[base + j]]
        cols.append(pg + offs)
        offs = offs + pg[:, page - 1:page]
    neg_c = -jnp.concatenate(cols, axis=1)
    terms = [t.astype(F32) for t in _split3(neg_c)] + [jnp.zeros_like(neg_c)]
    o_ref[...] = jnp.concatenate(terms, axis=0).astype(BF16)


def _fox_cache_bias(page_table, page_cs, layer, b):
    _, n_phys, nh, page = page_cs.shape
    n_pages = page_table.shape[0] // b
    grid_spec = pltpu.PrefetchScalarGridSpec(
        num_scalar_prefetch=1, grid=(b,),
        in_specs=[pl.BlockSpec((None, n_phys, nh, page), lambda i, pt: (layer, 0, 0, 0))],
        out_specs=pl.BlockSpec((None, 4 * nh, n_pages * page), lambda i, pt: (i, 0, 0)))
    return pl.pallas_call(
        functools.partial(_fox_cache_bias_kernel, n_pages=n_pages), grid_spec=grid_spec,
        out_shape=jax.ShapeDtypeStruct((b, 4 * nh, n_pages * page), BF16),
        compiler_params=_params(("arbitrary",)), name="fox_cache_bias",
    )(page_table, page_cs)


def _sample_attn(page_table, q, k_new, v_new, cache_kt, cache_vt, layer, lf_new=None, neg_c=None):
    fox = lf_new is not None
    b, tq, w = q.shape
    nh = w // HEAD_DIM
    page = cache_kt.shape[3]
    npg = PAGES_PER_STEP
    n_pages = page_table.shape[0] // b
    n_chunk = n_pages // npg
    n_blk = n_pages * page // MOBA_BLOCK
    nrow = SCORE_COLS

    new_spec = pl.BlockSpec((None, None, tq, w), lambda i, c, pt: (layer, i, 0, 0))
    in_specs = [pl.BlockSpec((None, tq, w), lambda i, c, pt: (i, 0, 0)), new_spec, new_spec]
    args = [q, k_new, v_new]
    if fox:
        in_specs.append(pl.BlockSpec((None, None, tq, nh), lambda i, c, pt: (layer, i, 0, 0)))
        in_specs.append(pl.BlockSpec((None, neg_c.shape[1], npg * page),
                                     lambda i, c, pt: (i, 0, jnp.minimum(c, n_chunk - 1))))
        args += [lf_new, neg_c]
    in_specs += [pl.BlockSpec(memory_space=pl.ANY)] * 2
    args += [cache_kt, cache_vt]
    blk_lanes = -(-n_blk // SCORE_COLS) * SCORE_COLS
    grid_spec = pltpu.PrefetchScalarGridSpec(
        num_scalar_prefetch=1, grid=(b, 2 * n_chunk), in_specs=in_specs,
        out_specs=pl.BlockSpec((None, tq, w), lambda i, c, pt: (i, 0, 0)),
        scratch_shapes=[pltpu.VMEM((nrow, w), BF16),
                        pltpu.VMEM((n_blk + 1, nrow, MOBA_BLOCK), F32), pltpu.VMEM((n_blk + 1, nrow, MOBA_BLOCK), BF16),
                        pltpu.VMEM((nrow, blk_lanes), F32), pltpu.VMEM((nrow, n_blk * SCORE_COLS), F32),
                        pltpu.VMEM((nrow, w), F32), pltpu.VMEM((nrow, 1), F32), pltpu.VMEM((nrow, 1), F32),
                        pltpu.VMEM((2, npg, w, page), F32), pltpu.SemaphoreType.DMA((2,))])
    return pl.pallas_call(
        functools.partial(_sample_attn_kernel, fox=fox, n_chunk=n_chunk, layer=layer, n_pages=n_pages),
        grid_spec=grid_spec,
        out_shape=jax.ShapeDtypeStruct((b, tq, w), BF16),
        compiler_params=_params(("arbitrary", "arbitrary")), name="fox_sample" if fox else "moba_sample",
    )(page_table, *args)


def _group_layer(x, layer, depth, wts, stacked, mixers, tm, final_norm, seq_len=None):
    b_f = wts["b_f"].reshape(-1, 1) if seq_len is not None else wts["b_f"].reshape(1, -1)
    qa, u, qc, stacked = _project(x, wts["norm1"], wts["w_t"], wts["wf_t"], b_f, stacked, layer, depth, tm, seq_len)
    a, b, c = mixers(qa, u, qc, stacked)
    x = _merge(x, wts["norm1"], a, b, c, wts["wg_t"], wts["w_a"], wts["w_b"], wts["w_c"], wts["w_o"], tm)
    x = _ffn(x, wts["norm2"], wts["w_fi"], wts["w_fo"], wts["norm_f"], final_norm, tm)
    return x, u, stacked


def kernel(x_prompt, x_sample, cache_a_k, cache_a_v, cache_c_k, cache_c_v, cache_c_logf, state_pool, page_table,
           norm1, w_in, b_f, w_pool, pool_scale, w_br_a, w_br_b, w_br_c, w_out, norm2, w_ffn_in, w_ffn_out, norm_f):
    bp, tp, d = x_prompt.shape
    bs, ts, _ = x_sample.shape
    depth, n_phys, page, nh, hd = cache_a_k.shape
    w = nh * hd
    n_state = state_pool.shape[2]
    past_len = page_table.shape[1] * page
    n_main = 7 * w

    paged = lambda cache: cache.transpose(0, 1, 3, 4, 2).reshape(depth, n_phys, w, page)
    cak, cav, cck, ccv = paged(cache_a_k), paged(cache_a_v), paged(cache_c_k), paged(cache_c_v)
    logf_rows = cache_c_logf.transpose(0, 1, 3, 2).reshape(depth * n_phys * nh, page)
    page_cs = _page_cumsum(logf_rows, 4096).reshape(depth, n_phys, nh, page)
    pt_flat = page_table.reshape(-1)
    hist_p = jnp.zeros((bp, POOL_HIST, w), F32)
    hist_s = jnp.pad(state_pool, ((0, 0), (0, 0), (POOL_HIST - n_state, 0), (0, 0)))
    w_in_t = jnp.swapaxes(w_in, 1, 2).astype(BF16)

    xp, xs = x_prompt.reshape(bp * tp, d), x_sample.reshape(bs * ts, d)
    st_p = st_s = None
    pool_p, pool_s = [], []
    for l in range(depth):
        wts = dict(norm1=norm1[l][None], norm2=norm2[l][None], norm_f=norm_f[None], b_f=b_f[l],
                   w_t=w_in_t[l, :n_main], wf_t=_pad_rows(w_in_t[l, n_main:n_main + nh], BF16_ROWS),
                   wg_t=w_in_t[l, n_main + nh:], w_a=w_br_a[l].astype(BF16),
                   w_b=w_br_b[l].astype(BF16), w_c=w_br_c[l].astype(BF16), w_o=w_out[l].astype(BF16),
                   w_fi=w_ffn_in[l].astype(BF16), w_fo=w_ffn_out[l].astype(BF16))
        wp, scale = w_pool[l].astype(BF16), pool_scale[l][None]
        last = l == depth - 1

        def prompt_mixers(qa, u, qc, st):
            ka_t, va_t, kc_t, vc_t, lf_t = st
            qa3, qc3 = qa.reshape(bp, tp, w), qc.reshape(bp, tp, w)
            a = _prompt_attn(qa3, ka_t, va_t, _moba_prep(qa3, ka_t, l), l, fox=False)
            b = _pool(u.reshape(bp, tp, w), hist_p, wp, scale, 0, 1)
            c = _prompt_attn(qc3, kc_t, vc_t, _fox_prep(lf_t, l), l, fox=True)
            return a.reshape(bp * tp, w), b.reshape(bp * tp, w), c.reshape(bp * tp, w)

        def sample_mixers(qa, u, qc, st):
            ka, va, kc, vc, lf = (s.reshape(depth, bs, ts, -1) for s in st)
            a = _sample_attn(pt_flat, qa.reshape(bs, ts, w), ka, va, cak, cav, l)
            b = _pool(u.reshape(bs, ts, w), hist_s[l], wp, scale, past_len, bs)
            neg_c = _fox_cache_bias(pt_flat, page_cs, l, bs)
            c = _sample_attn(pt_flat, qc.reshape(bs, ts, w), kc, vc, cck, ccv, l, lf_new=lf, neg_c=neg_c)
            return a.reshape(bs * ts, w), b.reshape(bs * ts, w), c.reshape(bs * ts, w)

        xs, u_s, st_s = _group_layer(xs, l, depth, wts, st_s, sample_mixers, bs * ts, last)
        xp, u_p, st_p = _group_layer(xp, l, depth, wts, st_p, prompt_mixers, 512, last, seq_len=tp)
        pool_p.append(u_p.reshape(bp, tp, w)[:, tp - n_state:])
        pool_s.append(jnp.concatenate([state_pool[l], u_s.reshape(bs, ts, w)], axis=1)[:, ts:])

    heads_p = lambda s: s.reshape(depth, bp, nh, hd, tp).transpose(0, 1, 4, 2, 3)
    heads_s = lambda s: s.reshape(depth, bs, ts, nh, hd)
    return (xp.reshape(bp, tp, d), xs.reshape(bs, ts, d),
            heads_p(st_p[0]), heads_p(st_p[1]), heads_p(st_p[2]), heads_p(st_p[3]),
            st_p[4].transpose(0, 1, 3, 2), jnp.stack(pool_p),
            heads_s(st_s[0]), heads_s(st_s[1]), heads_s(st_s[2]), heads_s(st_s[3]),
            st_s[4].reshape(depth, bs, ts, nh), jnp.stack(pool_s))
```

```python
import functools

import jax
import jax.numpy as jnp
from jax import lax
from jax.experimental import pallas as pl
from jax.experimental.pallas import tpu as pltpu

F32 = jnp.float32
BF16 = jnp.bfloat16

HEAD_DIM = 64
PAIR_W = 2 * HEAD_DIM
MOBA_BLOCK = 256
MOBA_TOPK = 3
POOL_WINDOWS = (2, 4, 8, 16)
POOL_GROUP_W = 128
POOL_HIST = max(POOL_WINDOWS)
RMS_EPS = 1e-6
ATTN_SCALE = HEAD_DIM ** -0.5
NEG = -1e30
QTILE = 256
PAGES_PER_STEP = 32
PAGE_SLOTS = 3
BF16_ROWS = 16
SCORE_COLS = 128
V7X_VMEM_LIMIT = 56 * 1024 * 1024


def _params(sem):
    return pltpu.CompilerParams(dimension_semantics=sem, vmem_limit_bytes=V7X_VMEM_LIMIT)


def _rmsnorm(x, g):
    ms = jnp.mean(x * x, axis=-1, keepdims=True)
    return x * lax.rsqrt(ms + RMS_EPS) * g


def _log_sigmoid(x):
    return -(jnp.maximum(-x, 0.0) + jnp.log1p(jnp.exp(-jnp.abs(x))))


def _dot(a, b):
    return jnp.dot(a, b, preferred_element_type=F32)


def _dot_nt(a, b, precision=None):
    return lax.dot_general(a, b, (((1,), (1,)), ((), ())), preferred_element_type=F32, precision=precision)


def _split3(x):
    hi = x.astype(BF16)
    r = x - hi.astype(F32)
    mid = r.astype(BF16)
    lo = (r - mid.astype(F32)).astype(BF16)
    return hi, mid, lo


def _dot3(a, x):
    hi, mid, lo = _split3(x)
    return _dot(a, hi) + _dot(a, mid) + _dot(a, lo)


def _pad_rows(x, rows):
    return jnp.concatenate([x, jnp.zeros((rows - x.shape[0],) + x.shape[1:], x.dtype)], axis=0)


def _upper_tri(n):
    r = lax.broadcasted_iota(jnp.int32, (n, n), 0)
    c = lax.broadcasted_iota(jnp.int32, (n, n), 1)
    return (r <= c).astype(BF16)


def _proj_kernel(*refs, n_alias, transposed):
    x_ref, g_ref, w_ref, wf_ref, bf_ref = refs[:5]
    qa_ref, ka_ref, va_ref, u_ref, qc_ref, kc_ref, vc_ref, lf_ref = refs[5 + n_alias:]
    w = qa_ref.shape[-1]
    nh = w // HEAD_DIM
    hb = _rmsnorm(x_ref[...], g_ref[...]).astype(BF16)
    for j, o_ref in ((0, qa_ref), (3, u_ref), (4, qc_ref)):
        o_ref[...] = _dot_nt(hb, w_ref[j * w:(j + 1) * w, :])
    if transposed:
        for j, k_ref, v_ref in ((1, ka_ref, va_ref), (5, kc_ref, vc_ref)):
            kv = _dot_nt(w_ref[j * w:(j + 2) * w, :], hb)
            k_ref[...] = kv[0:w]
            v_ref[...] = kv[w:2 * w]
        lf_ref[...] = _log_sigmoid(_dot_nt(wf_ref[...], hb)[0:nh] + bf_ref[...])
    else:
        for j, o_ref in ((1, ka_ref), (2, va_ref), (5, kc_ref), (6, vc_ref)):
            o_ref[...] = _dot_nt(hb, w_ref[j * w:(j + 1) * w, :])
        lf_ref[...] = _log_sigmoid(_dot_nt(hb, wf_ref[...])[:, 0:nh] + bf_ref[...])


def _project(x, g, w_t, wf_t, b_f, stacked, layer, depth, tm, seq_len=None):
    m, d = x.shape
    w = w_t.shape[0] // 7
    nh = w // HEAD_DIM
    transposed = seq_len is not None
    row = lambda i: (i, 0)
    const = lambda i: (0, 0)
    in_specs = [pl.BlockSpec((tm, d), row), pl.BlockSpec((1, d), const), pl.BlockSpec(w_t.shape, const),
                pl.BlockSpec(wf_t.shape, const), pl.BlockSpec(b_f.shape, const)]
    args = [x, g, w_t, wf_t, b_f]
    aliases = {}
    if stacked is not None:
        in_specs += [pl.BlockSpec(memory_space=pl.ANY)] * 5
        args += list(stacked)
        aliases = {5: 1, 6: 2, 7: 5, 8: 6, 9: 7}
    plain = pl.BlockSpec((tm, w), row)
    plain_s = jax.ShapeDtypeStruct((m, w), F32)
    if transposed:
        nt = seq_len // tm
        lay = lambda i: (layer, i // nt, 0, i % nt)
        stack, lf_spec = pl.BlockSpec((None, None, w, tm), lay), pl.BlockSpec((None, None, nh, tm), lay)
        stack_s = jax.ShapeDtypeStruct((depth, m // seq_len, w, seq_len), F32)
        lf_s = jax.ShapeDtypeStruct((depth, m // seq_len, nh, seq_len), F32)
    else:
        lay = lambda i: (layer, i, 0)
        stack, lf_spec = pl.BlockSpec((None, tm, w), lay), pl.BlockSpec((None, tm, nh), lay)
        stack_s = jax.ShapeDtypeStruct((depth, m, w), F32)
        lf_s = jax.ShapeDtypeStruct((depth, m, nh), F32)
    qa, ka, va, u, qc, kc, vc, lf = pl.pallas_call(
        functools.partial(_proj_kernel, n_alias=len(aliases), transposed=transposed), grid=(m // tm,),
        in_specs=in_specs, out_specs=[plain, stack, stack, plain, plain, stack, stack, lf_spec],
        out_shape=[plain_s, stack_s, stack_s, plain_s, plain_s, stack_s, stack_s, lf_s],
        input_output_aliases=aliases, compiler_params=_params(("arbitrary",)), name="proj",
    )(*args)
    return qa, u, qc, (ka, va, kc, vc, lf)


def _merge_kernel(x_ref, g_ref, a_ref, b_ref, c_ref, wg_ref, wa_ref, wb_ref, wc_ref, wo_ref, o_ref):
    x = x_ref[...]
    d = x.shape[-1]
    hb = _rmsnorm(x, g_ref[...]).astype(BF16)
    y = None
    for i, (br_ref, w_ref) in enumerate(((a_ref, wa_ref), (b_ref, wb_ref), (c_ref, wc_ref))):
        gate = jax.nn.sigmoid(_dot_nt(hb, wg_ref[i * d:(i + 1) * d, :]))
        t = gate * _dot(br_ref[...], w_ref[...])
        y = t if y is None else y + t
    o_ref[...] = x + _dot(y.astype(BF16), wo_ref[...])


def _merge(x, g, a, b, c, wg_t, w_a, w_b, w_c, w_o, tm):
    m, d = x.shape
    row = lambda i: (i, 0)
    const = lambda i: (0, 0)
    full = lambda arr: pl.BlockSpec(arr.shape, const)
    br = lambda arr: pl.BlockSpec((tm, arr.shape[1]), row)
    return pl.pallas_call(
        _merge_kernel, grid=(m // tm,),
        in_specs=[pl.BlockSpec((tm, d), row), pl.BlockSpec((1, d), const), br(a), br(b), br(c),
                  full(wg_t), full(w_a), full(w_b), full(w_c), full(w_o)],
        out_specs=pl.BlockSpec((tm, d), row), out_shape=jax.ShapeDtypeStruct((m, d), F32),
        compiler_params=_params(("arbitrary",)), name="merge",
    )(x, g, a, b, c, wg_t, w_a, w_b, w_c, w_o)


def _ffn_kernel(x_ref, g_ref, wi_ref, wo_ref, gf_ref, o_ref, *, chunk, final_norm):
    x = x_ref[...]
    d_ff = wo_ref.shape[0]
    hb = _rmsnorm(x, g_ref[...]).astype(BF16)
    acc = x
    for c0 in range(0, d_ff, chunk):
        gt = _dot(hb, wi_ref[:, c0:c0 + chunk])
        up = _dot(hb, wi_ref[:, d_ff + c0:d_ff + c0 + chunk])
        act = (gt * jax.nn.sigmoid(gt) * up).astype(BF16)
        acc = acc + _dot(act, wo_ref[c0:c0 + chunk, :])
    o_ref[...] = _rmsnorm(acc, gf_ref[...]) if final_norm else acc


def _ffn(x, g, w_i, w_o, g_final, final_norm, tm, chunk=256):
    m, d = x.shape
    row = lambda i: (i, 0)
    const = lambda i: (0, 0)
    return pl.pallas_call(
        functools.partial(_ffn_kernel, chunk=chunk, final_norm=final_norm), grid=(m // tm,),
        in_specs=[pl.BlockSpec((tm, d), row), pl.BlockSpec((1, d), const), pl.BlockSpec(w_i.shape, const),
                  pl.BlockSpec(w_o.shape, const), pl.BlockSpec((1, d), const)],
        out_specs=pl.BlockSpec((tm, d), row), out_shape=jax.ShapeDtypeStruct((m, d), F32),
        compiler_params=_params(("arbitrary",)), name="ffn",
    )(x, g, w_i, w_o, g_final)


def _pool_kernel(u_ref, hist_ref, w_ref, scale_ref, o_ref, s_ref, *, pos0):
    nb, t, _ = u_ref.shape
    ext = t + POOL_HIST
    pos = pos0 + lax.broadcasted_iota(jnp.int32, (1, t, 1), 1)
    for gi, win in enumerate(POOL_WINDOWS):
        sl = slice(gi * POOL_GROUP_W, (gi + 1) * POOL_GROUP_W)
        u = u_ref[:, :, sl]
        s_ref[:, 0:POOL_HIST, :] = hist_ref[:, :, sl]
        s_ref[:, POOL_HIST:, :] = u
        shift = 1
        while shift < win:
            s_ref[:, shift:, :] = s_ref[:, shift:, :] + s_ref[:, 0:ext - shift, :]
            shift *= 2
        cnt = jnp.minimum(win, pos + 1).astype(F32)
        dlt = s_ref[:, POOL_HIST:, :] / cnt - u
        y = _dot(dlt.reshape(nb * t, POOL_GROUP_W).astype(BF16), w_ref[gi])
        o_ref[:, :, sl] = (y.reshape(nb, t, POOL_GROUP_W) * scale_ref[:, sl]).astype(o_ref.dtype)


def _pool(u, hist, w_pool, scale, pos0, nb):
    b, t, w = u.shape
    blk = lambda i: (i, 0, 0)
    return pl.pallas_call(
        functools.partial(_pool_kernel, pos0=pos0), grid=(b // nb,),
        in_specs=[pl.BlockSpec((nb, t, w), blk), pl.BlockSpec((nb, POOL_HIST, w), blk),
                  pl.BlockSpec(w_pool.shape, lambda i: (0, 0, 0)), pl.BlockSpec((1, w), lambda i: (0, 0))],
        out_specs=pl.BlockSpec((nb, t, w), blk), out_shape=jax.ShapeDtypeStruct((b, t, w), BF16),
        scratch_shapes=[pltpu.VMEM((nb, t + POOL_HIST, POOL_GROUP_W), F32)],
        compiler_params=_params(("arbitrary",)), name="pool",
    )(u, hist, w_pool, scale)


def _fox_prep_kernel(lf_ref, kx_ref):
    nh, t = lf_ref.shape
    tri = _upper_tri(QTILE)
    carry = jnp.zeros((BF16_ROWS, 1), F32)
    parts = []
    for blk in range(t // QTILE):
        x = _pad_rows(lf_ref[:, blk * QTILE:(blk + 1) * QTILE], BF16_ROWS)
        hi, mid, lo = _split3(x)
        cs = _dot(hi, tri) + _dot(mid, tri) + _dot(lo, tri) + carry
        carry = cs[:, QTILE - 1:QTILE]
        parts.append(cs)
    neg_c = -jnp.concatenate(parts, axis=1)
    terms = [term.astype(F32) for term in _split3(neg_c)]
    row = lax.broadcasted_iota(jnp.int32, (BF16_ROWS, t), 0)
    for p in range(nh // 2):
        out = jnp.zeros((BF16_ROWS, t), F32)
        for head in range(2):
            for part in range(3):
                src = terms[part][2 * p + head:2 * p + head + 1, :]
                out = jnp.where(row == 3 * head + part, src, out)
        kx_ref[p] = out.astype(BF16)


def _fox_prep(logf_t, layer):
    _, b, nh, t = logf_t.shape
    return pl.pallas_call(
        _fox_prep_kernel, grid=(b,),
        in_specs=[pl.BlockSpec((None, None, nh, t), lambda i: (layer, i, 0, 0))],
        out_specs=pl.BlockSpec((None, nh // 2, BF16_ROWS, t), lambda i: (i, 0, 0, 0)),
        out_shape=jax.ShapeDtypeStruct((b, nh // 2, BF16_ROWS, t), BF16),
        compiler_params=_params(("arbitrary",)), name="fox_prep",
    )(logf_t)


def _moba_prep_kernel(q_ref, kt_ref, qx_ref):
    t, w = q_ref.shape
    nh, nb = w // HEAD_DIM, t // MOBA_BLOCK
    lane = lax.broadcasted_iota(jnp.int32, (w, PAIR_W), 1)
    kbar_t = jnp.zeros((w, PAIR_W), F32)
    for n in range(nb):
        col = jnp.mean(kt_ref[:, n * MOBA_BLOCK:(n + 1) * MOBA_BLOCK], axis=1, keepdims=True)
        kbar_t = jnp.where(lane == n, col, kbar_t)
    kbar = kbar_t.T[0:nb]
    head_of_lane = lax.broadcasted_iota(jnp.int32, (nh, nb, w), 2) // HEAD_DIM
    head = lax.broadcasted_iota(jnp.int32, (nh, nb, w), 0)
    kb = jnp.where(head_of_lane == head, kbar[None], 0.0).reshape(nh * nb, w)
    s = _dot_nt(kb, q_ref[...], precision=lax.Precision.HIGHEST).reshape(nh, nb, t)
    own = lax.broadcasted_iota(jnp.int32, (nh, nb, t), 2) // MOBA_BLOCK
    n_idx = lax.broadcasted_iota(jnp.int32, (nh, nb, t), 1)
    past = n_idx < own
    s = jnp.where(past, s, -jnp.inf)
    rank = jnp.zeros((nh, nb, t), jnp.int32)
    for m in range(nb):
        sm = s[:, m:m + 1, :]
        rank = rank + ((sm > s) | ((sm == s) & (m < n_idx))).astype(jnp.int32)
    keep = (past & (rank < MOBA_TOPK)) | (n_idx >= own)
    bias = jnp.where(keep, 0.0, NEG).reshape(nh * nb, t)
    for p in range(nh // 2):
        qx_ref[p] = _pad_rows(bias[2 * p * nb:(2 * p + 2) * nb], PAIR_W).T.astype(BF16)


def _moba_prep(q, k_t, layer):
    b, t, w = q.shape
    n_pair = w // PAIR_W
    return pl.pallas_call(
        _moba_prep_kernel, grid=(b,),
        in_specs=[pl.BlockSpec((None, t, w), lambda i: (i, 0, 0)),
                  pl.BlockSpec((None, None, w, t), lambda i: (layer, i, 0, 0))],
        out_specs=pl.BlockSpec((None, n_pair, t, PAIR_W), lambda i: (i, 0, 0, 0)),
        out_shape=jax.ShapeDtypeStruct((b, n_pair, t, PAIR_W), BF16),
        compiler_params=_params(("arbitrary",)), name="moba_prep",
    )(q, k_t)


def _prompt_attn_kernel(q_ref, kt_ref, vt_ref, x_ref, o_ref, qs_ref, kp_ref, vb_ref, *, fox):
    t = q_ref.shape[0]
    nt = t // QTILE
    lane = lax.broadcasted_iota(jnp.int32, (t, PAIR_W), 1)
    first = lane < HEAD_DIM
    qs = q_ref[...] * ATTN_SCALE
    heads_q = (jnp.where(first, qs, 0.0).astype(BF16), jnp.where(first, 0.0, qs).astype(BF16))
    if fox:
        tile_lane = lax.broadcasted_iota(jnp.int32, (QTILE, PAIR_W), 1)
        ones_x = (jnp.where(tile_lane < 3, 1.0, 0.0).astype(BF16),
                  jnp.where((tile_lane >= 3) & (tile_lane < 6), 1.0, 0.0).astype(BF16))
        heads_x = None
        kx_t = _pad_rows(x_ref[...].astype(F32), PAIR_W)
        kp_ref[...] = jnp.concatenate([kt_ref[...], kx_t], axis=0).T.astype(BF16)
    else:
        x = x_ref[...]
        zero = jnp.zeros_like(x)
        heads_x = (jnp.where(lane < nt, x, zero), jnp.where((lane >= nt) & (lane < 2 * nt), x, zero))
        row = lax.broadcasted_iota(jnp.int32, (t, PAIR_W), 0)
        kp_ref[:, 0:PAIR_W] = kt_ref[...].T.astype(BF16)
        kp_ref[:, PAIR_W:] = jnp.where((lane < 2 * nt) & (lane % nt == row // MOBA_BLOCK), 1.0, 0.0).astype(BF16)
    vb_ref[...] = vt_ref[...].astype(BF16)
    for i in range(nt):
        for h in range(2):
            qs_ref[i, h * QTILE:(h + 1) * QTILE, 0:PAIR_W] = heads_q[h][i * QTILE:(i + 1) * QTILE]
            qs_ref[i, h * QTILE:(h + 1) * QTILE, PAIR_W:] = (
                ones_x[h] if fox else heads_x[h][i * QTILE:(i + 1) * QTILE])

    key = lax.broadcasted_iota(jnp.int32, (QTILE, 2 * QTILE), 0)
    qry = lax.broadcasted_iota(jnp.int32, (QTILE, 2 * QTILE), 1) % QTILE
    causal = key <= qry
    first_out = lax.broadcasted_iota(jnp.int32, (PAIR_W, QTILE), 0) < HEAD_DIM
    def scores(i):
        lo = i * QTILE
        qi = qs_ref[i]
        s_diag = jnp.where(causal, _dot_nt(kp_ref[lo:lo + QTILE, :], qi), NEG)
        s_past = _dot_nt(kp_ref[0:lo, :], qi) if i > 0 else None
        return s_diag, s_past

    def probs(i, s_diag, s_past):
        m = jnp.max(s_diag, axis=0, keepdims=True)
        if i > 0:
            m = jnp.maximum(m, jnp.max(s_past, axis=0, keepdims=True))
        p_diag = jnp.exp(s_diag - m)
        l = jnp.sum(p_diag, axis=0, keepdims=True)
        p_past = None
        if i > 0:
            p_past = jnp.exp(s_past - m)
            l = l + jnp.sum(p_past, axis=0, keepdims=True)
            p_past = p_past.astype(BF16)
        return p_diag.astype(BF16), p_past, l

    def values(i, p_diag, p_past, l):
        lo = i * QTILE
        pv = _dot(vb_ref[:, lo:lo + QTILE], p_diag)
        if i > 0:
            pv = pv + _dot(vb_ref[:, 0:lo], p_past)
        o = pv / l
        o_ref[lo:lo + QTILE, :] = jnp.where(first_out, o[:, 0:QTILE], o[:, QTILE:]).T.astype(o_ref.dtype)

    pairs = [(a, nt - 1 - a) for a in range(nt // 2)]
    pending = None
    for pair in pairs:
        sc = [scores(i) for i in pair]
        if pending is not None:
            for i, p in pending:
                values(i, *p)
        pending = [(i, probs(i, *s)) for i, s in zip(pair, sc)]
    for i, p in pending:
        values(i, *p)


def _prompt_attn(q, k_t, v_t, x, layer, fox):
    b, t, w = q.shape
    n_pair = w // PAIR_W
    kv_spec = pl.BlockSpec((None, None, PAIR_W, t), lambda i, p: (layer, i, p, 0))
    x_spec = pl.BlockSpec((None, None) + x.shape[2:], lambda i, p: (i, p, 0, 0))
    return pl.pallas_call(
        functools.partial(_prompt_attn_kernel, fox=fox), grid=(b, n_pair),
        in_specs=[pl.BlockSpec((None, t, PAIR_W), lambda i, p: (i, 0, p)), kv_spec, kv_spec, x_spec],
        out_specs=pl.BlockSpec((None, t, PAIR_W), lambda i, p: (i, 0, p)),
        out_shape=jax.ShapeDtypeStruct((b, t, w), BF16),
        scratch_shapes=[pltpu.VMEM((t // QTILE, 2 * QTILE, 2 * PAIR_W), BF16),
                        pltpu.VMEM((t, 2 * PAIR_W), BF16), pltpu.VMEM((PAIR_W, t), BF16)],
        compiler_params=_params(("arbitrary", "arbitrary")), name="fox_attn" if fox else "moba_attn",
    )(q, k_t, v_t, x)


def _page_cumsum_kernel(x_ref, o_ref):
    o_ref[...] = _dot3_right(x_ref[...], _upper_tri(x_ref.shape[1]))


def _dot3_right(x, a):
    hi, mid, lo = _split3(x)
    return _dot(hi, a) + _dot(mid, a) + _dot(lo, a)


def _page_cumsum(logf_rows, tm):
    m, n = logf_rows.shape
    return pl.pallas_call(
        _page_cumsum_kernel, grid=(m // tm,),
        in_specs=[pl.BlockSpec((tm, n), lambda i: (i, 0))], out_specs=pl.BlockSpec((tm, n), lambda i: (i, 0)),
        out_shape=jax.ShapeDtypeStruct((m, n), F32),
        compiler_params=_params(("arbitrary",)), name="page_cumsum",
    )(logf_rows)


def _sample_attn_kernel(pt_ref, q_ref, kn_ref, vn_ref, *rest, fox, n_chunk, layer, n_pages):
    if fox:
        lfn_ref, negc_ref, rest = rest[0], rest[1], rest[2:]
    (ck_hbm, cv_hbm, o_ref, qb_ref, sc_ref, p_ref, sblk_ref, brep_ref, acc_ref, l_ref, ctot_ref,
     buf_ref, sem_ref) = rest
    tq, w = q_ref.shape
    nh = w // HEAD_DIM
    nrow = qb_ref.shape[0]
    npg, page = buf_ref.shape[1], buf_ref.shape[3]
    pg_per_blk = MOBA_BLOCK // page
    blk_per_step = npg // pg_per_blk
    n_blk = n_chunk * blk_per_step
    n_step = 2 * n_chunk
    i, c = pl.program_id(0), pl.program_id(1)
    n_slots = buf_ref.shape[0]
    ahead = n_slots - 1
    g = i * n_step + c
    slot = lax.rem(g, n_slots)

    def page_copy(src_hbm, phys_page, sl, r):
        return pltpu.make_async_copy(src_hbm.at[layer, phys_page], buf_ref.at[sl, r], sem_ref.at[sl])

    def start_step(gs):
        seq, step, sl = lax.div(gs, n_step), lax.rem(gs, n_step), lax.rem(gs, n_slots)
        first = seq * n_pages + lax.rem(step, n_chunk) * npg

        @pl.when(step < n_chunk)
        def _():
            for r in range(npg):
                page_copy(ck_hbm, pt_ref[first + r], sl, r).start()

        @pl.when(step >= n_chunk)
        def _():
            for r in range(npg):
                page_copy(cv_hbm, pt_ref[first + r], sl, r).start()

    @pl.when(g == 0)
    def _():
        for first_steps in range(ahead):
            start_step(jnp.int32(first_steps))

    @pl.when(g + ahead < pl.num_programs(0) * n_step)
    def _():
        start_step(g + ahead)

    for r in range(npg):
        page_copy(ck_hbm, 0, slot, r).wait()
    k_pages = v_pages = [buf_ref.at[slot, r] for r in range(npg)]

    @pl.when(c == 0)
    def _():
        rows = lax.broadcasted_iota(jnp.int32, (nrow, w), 0)
        lanes = lax.broadcasted_iota(jnp.int32, (nrow, w), 1)
        qrep = jnp.concatenate([q_ref[...]] * (nrow // tq), axis=0)
        qbd = jnp.where((rows // tq == lanes // HEAD_DIM) & (rows < nh * tq), qrep, 0.0)
        qb_ref[...] = (qbd * ATTN_SCALE).astype(BF16)
        acc_ref[...] = jnp.zeros_like(acc_ref)
        sblk_ref[...] = jnp.zeros_like(sblk_ref)

    @pl.when(c < n_chunk)
    def _():
        qb = qb_ref[...]
        if fox:
            erow = lax.broadcasted_iota(jnp.int32, (nrow, negc_ref.shape[0]), 0)
            ecol = lax.broadcasted_iota(jnp.int32, (nrow, negc_ref.shape[0]), 1)
            expand = ((ecol < 3 * nh) & (ecol % nh == erow // tq) & (erow < nh * tq)).astype(BF16)
        else:
            blk_lane = lax.broadcasted_iota(jnp.int32, (nrow, sblk_ref.shape[1]), 1)
            sblk = sblk_ref[...]
        for j in range(blk_per_step):
            pages = range(j * pg_per_blk, (j + 1) * pg_per_blk)
            kt = jnp.concatenate([k_pages[r][...] for r in pages], axis=1)
            s = _dot(qb, kt.astype(BF16))
            if fox:
                bias = _dot(expand, negc_ref[:, j * MOBA_BLOCK:(j + 1) * MOBA_BLOCK])
                s = s + bias
                if j == blk_per_step - 1:
                    ctot_ref[...] = bias[:, MOBA_BLOCK - 1:MOBA_BLOCK]
            else:
                half = s[:, 0:MOBA_BLOCK // 2] + s[:, MOBA_BLOCK // 2:]
                mean = jnp.sum(half, axis=1, keepdims=True) * (1.0 / MOBA_BLOCK)
                sblk = jnp.where(blk_lane == c * blk_per_step + j, mean, sblk)
            sc_ref[c * blk_per_step + j] = s
        if not fox:
            sblk_ref[...] = sblk

    @pl.when(c == n_chunk - 1)
    def _():
        s_new = _dot_nt(qb_ref[...], _pad_rows(kn_ref[...], MOBA_BLOCK).astype(BF16))
        key = lax.broadcasted_iota(jnp.int32, (nrow, MOBA_BLOCK), 1)
        qry = lax.broadcasted_iota(jnp.int32, (nrow, MOBA_BLOCK), 0) % tq
        if fox:
            lane8 = lax.broadcasted_iota(jnp.int32, (nrow, nh), 1)
            head8 = lax.broadcasted_iota(jnp.int32, (nrow, nh), 0) // tq
            cs = jnp.zeros((1, nh), F32)
            bias = jnp.broadcast_to(ctot_ref[...], (nrow, MOBA_BLOCK))
            for i in range(tq):
                cs = cs + lfn_ref[i:i + 1, :]
                col = jnp.sum(jnp.where(lane8 == head8, jnp.broadcast_to(cs, (nrow, nh)), 0.0), axis=1, keepdims=True)
                bias = jnp.where(key == i, bias - col, bias)
            s_new = s_new + bias
        sc_ref[n_blk] = jnp.where((key <= qry) & (key < tq), s_new, NEG)

        if not fox:
            sb = sblk_ref[...].T[0:n_blk]
            n_idx = lax.broadcasted_iota(jnp.int32, sb.shape, 0)
            rank = jnp.zeros(sb.shape, jnp.int32)
            for m_i in range(n_blk):
                sm = sb[m_i:m_i + 1, :]
                rank = rank + ((sm > sb) | ((sm == sb) & (m_i < n_idx))).astype(jnp.int32)
            bias_t = _pad_rows(jnp.where(rank < MOBA_TOPK, 0.0, NEG), sblk_ref.shape[1]).T
            rep_r = lax.broadcasted_iota(jnp.int32, (sblk_ref.shape[1], brep_ref.shape[1]), 0)
            rep_c = lax.broadcasted_iota(jnp.int32, (sblk_ref.shape[1], brep_ref.shape[1]), 1)
            brep_ref[...] = _dot(bias_t.astype(BF16), (rep_r == rep_c // SCORE_COLS).astype(BF16))

        def block(n):
            s = sc_ref[n]
            if not fox and n < n_blk:
                b = brep_ref[:, n * SCORE_COLS:(n + 1) * SCORE_COLS]
                s = s + jnp.concatenate([b] * (MOBA_BLOCK // SCORE_COLS), axis=1)
            return s

        mx = block(0)
        for n in range(1, n_blk + 1):
            mx = jnp.maximum(mx, block(n))
        m = jnp.broadcast_to(jnp.max(mx, axis=1, keepdims=True), mx.shape)
        tot = jnp.zeros_like(mx)
        for n in range(n_blk + 1):
            p = jnp.exp(block(n) - m)
            tot = tot + p
            p_ref[n] = p.astype(BF16)
        l_ref[...] = jnp.sum(tot, axis=1, keepdims=True)

    @pl.when(c >= n_chunk)
    def _():
        acc = acc_ref[...]
        for j in range(blk_per_step):
            pages = range(j * pg_per_blk, (j + 1) * pg_per_blk)
            vt = jnp.concatenate([v_pages[r][...] for r in pages], axis=1).astype(BF16)
            acc = acc + _dot_nt(p_ref[(c - n_chunk) * blk_per_step + j], vt)
        acc_ref[...] = acc

    @pl.when(c == 2 * n_chunk - 1)
    def _():
        acc = acc_ref[...] + _dot(p_ref[n_blk], _pad_rows(vn_ref[...], MOBA_BLOCK).astype(BF16))
        o_full = acc / l_ref[...]
        lane_head = lax.broadcasted_iota(jnp.int32, (tq, w), 1) // HEAD_DIM
        out = jnp.zeros((tq, w), F32)
        for h in range(nh):
            out = out + jnp.where(lane_head == h, o_full[h * tq:(h + 1) * tq, :], 0.0)
        o_ref[...] = out.astype(o_ref.dtype)


def _fox_cache_bias_kernel(pt_ref, pcs_ref, o_ref, *, n_pages):
    nh, page = pcs_ref.shape[1:]
    base = pl.program_id(0) * n_pages
    offs = jnp.zeros((nh, 1), F32)
    cols = []
    for j in range(n_pages):
        pg = pcs_ref[pt_ref[base + j]]
        cols.append(pg + offs)
        offs = offs + pg[:, page - 1:page]
    neg_c = -jnp.concatenate(cols, axis=1)
    terms = [t.astype(F32) for t in _split3(neg_c)] + [jnp.zeros_like(neg_c)]
    o_ref[...] = jnp.concatenate(terms, axis=0).astype(BF16)


def _fox_cache_bias(page_table, page_cs, layer, b):
    _, n_phys, nh, page = page_cs.shape
    n_pages = page_table.shape[0] // b
    grid_spec = pltpu.PrefetchScalarGridSpec(
        num_scalar_prefetch=1, grid=(b,),
        in_specs=[pl.BlockSpec((None, n_phys, nh, page), lambda i, pt: (layer, 0, 0, 0))],
        out_specs=pl.BlockSpec((None, 4 * nh, n_pages * page), lambda i, pt: (i, 0, 0)))
    return pl.pallas_call(
        functools.partial(_fox_cache_bias_kernel, n_pages=n_pages), grid_spec=grid_spec,
        out_shape=jax.ShapeDtypeStruct((b, 4 * nh, n_pages * page), BF16),
        compiler_params=_params(("arbitrary",)), name="fox_cache_bias",
    )(page_table, page_cs)


def _sample_attn(page_table, q, k_new, v_new, cache_kt, cache_vt, layer, lf_new=None, neg_c=None):
    fox = lf_new is not None
    b, tq, w = q.shape
    nh = w // HEAD_DIM
    page = cache_kt.shape[3]
    npg = PAGES_PER_STEP
    n_pages = page_table.shape[0] // b
    n_chunk = n_pages // npg
    n_blk = n_pages * page // MOBA_BLOCK
    nrow = SCORE_COLS

    new_spec = pl.BlockSpec((None, None, tq, w), lambda i, c, pt: (layer, i, 0, 0))
    in_specs = [pl.BlockSpec((None, tq, w), lambda i, c, pt: (i, 0, 0)), new_spec, new_spec]
    args = [q, k_new, v_new]
    if fox:
        in_specs.append(pl.BlockSpec((None, None, tq, nh), lambda i, c, pt: (layer, i, 0, 0)))
        in_specs.append(pl.BlockSpec((None, neg_c.shape[1], npg * page),
                                     lambda i, c, pt: (i, 0, jnp.minimum(c, n_chunk - 1))))
        args += [lf_new, neg_c]
    in_specs += [pl.BlockSpec(memory_space=pl.ANY)] * 2
    args += [cache_kt, cache_vt]
    blk_lanes = -(-n_blk // SCORE_COLS) * SCORE_COLS
    grid_spec = pltpu.PrefetchScalarGridSpec(
        num_scalar_prefetch=1, grid=(b, 2 * n_chunk), in_specs=in_specs,
        out_specs=pl.BlockSpec((None, tq, w), lambda i, c, pt: (i, 0, 0)),
        scratch_shapes=[pltpu.VMEM((nrow, w), BF16),
                        pltpu.VMEM((n_blk + 1, nrow, MOBA_BLOCK), F32), pltpu.VMEM((n_blk + 1, nrow, MOBA_BLOCK), BF16),
                        pltpu.VMEM((nrow, blk_lanes), F32), pltpu.VMEM((nrow, n_blk * SCORE_COLS), F32),
                        pltpu.VMEM((nrow, w), F32), pltpu.VMEM((nrow, 1), F32), pltpu.VMEM((nrow, 1), F32),
                        pltpu.VMEM((PAGE_SLOTS, npg, w, page), F32), pltpu.SemaphoreType.DMA((PAGE_SLOTS,))])
    return pl.pallas_call(
        functools.partial(_sample_attn_kernel, fox=fox, n_chunk=n_chunk, layer=layer, n_pages=n_pages),
        grid_spec=grid_spec,
        out_shape=jax.ShapeDtypeStruct((b, tq, w), BF16),
        compiler_params=_params(("arbitrary", "arbitrary")), name="fox_sample" if fox else "moba_sample",
    )(page_table, *args)


def _group_layer(x, layer, depth, wts, stacked, mixers, tm, final_norm, seq_len=None):
    b_f = wts["b_f"].reshape(-1, 1) if seq_len is not None else wts["b_f"].reshape(1, -1)
    qa, u, qc, stacked = _project(x, wts["norm1"], wts["w_t"], wts["wf_t"], b_f, stacked, layer, depth, tm, seq_len)
    a, b, c = mixers(qa, u, qc, stacked)
    x = _merge(x, wts["norm1"], a, b, c, wts["wg_t"], wts["w_a"], wts["w_b"], wts["w_c"], wts["w_o"], tm)
    x = _ffn(x, wts["norm2"], wts["w_fi"], wts["w_fo"], wts["norm_f"], final_norm, tm)
    return x, u, stacked


def kernel(x_prompt, x_sample, cache_a_k, cache_a_v, cache_c_k, cache_c_v, cache_c_logf, state_pool, page_table,
           norm1, w_in, b_f, w_pool, pool_scale, w_br_a, w_br_b, w_br_c, w_out, norm2, w_ffn_in, w_ffn_out, norm_f):
    bp, tp, d = x_prompt.shape
    bs, ts, _ = x_sample.shape
    depth, n_phys, page, nh, hd = cache_a_k.shape
    w = nh * hd
    n_state = state_pool.shape[2]
    past_len = page_table.shape[1] * page
    n_main = 7 * w

    paged = lambda cache: cache.transpose(0, 1, 3, 4, 2).reshape(depth, n_phys, w, page)
    cak, cav, cck, ccv = paged(cache_a_k), paged(cache_a_v), paged(cache_c_k), paged(cache_c_v)
    logf_rows = cache_c_logf.transpose(0, 1, 3, 2).reshape(depth * n_phys * nh, page)
    page_cs = _page_cumsum(logf_rows, 4096).reshape(depth, n_phys, nh, page)
    pt_flat = page_table.reshape(-1)
    hist_p = jnp.zeros((bp, POOL_HIST, w), F32)
    hist_s = jnp.pad(state_pool, ((0, 0), (0, 0), (POOL_HIST - n_state, 0), (0, 0)))
    w_in_t = jnp.swapaxes(w_in, 1, 2).astype(BF16)

    xp, xs = x_prompt.reshape(bp * tp, d), x_sample.reshape(bs * ts, d)
    st_p = st_s = None
    pool_p, pool_s = [], []
    for l in range(depth):
        wts = dict(norm1=norm1[l][None], norm2=norm2[l][None], norm_f=norm_f[None], b_f=b_f[l],
                   w_t=w_in_t[l, :n_main], wf_t=_pad_rows(w_in_t[l, n_main:n_main + nh], BF16_ROWS),
                   wg_t=w_in_t[l, n_main + nh:], w_a=w_br_a[l].astype(BF16),
                   w_b=w_br_b[l].astype(BF16), w_c=w_br_c[l].astype(BF16), w_o=w_out[l].astype(BF16),
                   w_fi=w_ffn_in[l].astype(BF16), w_fo=w_ffn_out[l].astype(BF16))
        wp, scale = w_pool[l].astype(BF16), pool_scale[l][None]
        last = l == depth - 1

        def prompt_mixers(qa, u, qc, st):
            ka_t, va_t, kc_t, vc_t, lf_t = st
            qa3, qc3 = qa.reshape(bp, tp, w), qc.reshape(bp, tp, w)
            a = _prompt_attn(qa3, ka_t, va_t, _moba_prep(qa3, ka_t, l), l, fox=False)
            b = _pool(u.reshape(bp, tp, w), hist_p, wp, scale, 0, 1)
            c = _prompt_attn(qc3, kc_t, vc_t, _fox_prep(lf_t, l), l, fox=True)
            return a.reshape(bp * tp, w), b.reshape(bp * tp, w), c.reshape(bp * tp, w)

        def sample_mixers(qa, u, qc, st):
            ka, va, kc, vc, lf = (s.reshape(depth, bs, ts, -1) for s in st)
            a = _sample_attn(pt_flat, qa.reshape(bs, ts, w), ka, va, cak, cav, l)
            b = _pool(u.reshape(bs, ts, w), hist_s[l], wp, scale, past_len, bs)
            neg_c = _fox_cache_bias(pt_flat, page_cs, l, bs)
            c = _sample_attn(pt_flat, qc.reshape(bs, ts, w), kc, vc, cck, ccv, l, lf_new=lf, neg_c=neg_c)
            return a.reshape(bs * ts, w), b.reshape(bs * ts, w), c.reshape(bs * ts, w)

        xs, u_s, st_s = _group_layer(xs, l, depth, wts, st_s, sample_mixers, bs * ts, last)
        xp, u_p, st_p = _group_layer(xp, l, depth, wts, st_p, prompt_mixers, 512, last, seq_len=tp)
        pool_p.append(u_p.reshape(bp, tp, w)[:, tp - n_state:])
        pool_s.append(jnp.concatenate([state_pool[l], u_s.reshape(bs, ts, w)], axis=1)[:, ts:])

    heads_p = lambda s: s.reshape(depth, bp, nh, hd, tp).transpose(0, 1, 4, 2, 3)
    heads_s = lambda s: s.reshape(depth, bs, ts, nh, hd)
    return (xp.reshape(bp, tp, d), xs.reshape(bs, ts, d),
            heads_p(st_p[0]), heads_p(st_p[1]), heads_p(st_p[2]), heads_p(st_p[3]),
            st_p[4].transpose(0, 1, 3, 2), jnp.stack(pool_p),
            heads_s(st_s[0]), heads_s(st_s[1]), heads_s(st_s[2]), heads_s(st_s[3]),
            st_s[4].reshape(depth, bs, ts, nh), jnp.stack(pool_s))
```
